```python
import math
import jax, jax.numpy as jnp
from jax import lax
import numpy as np

D_MODEL = 2048
BATCH = 4
SEQ = 4096
DEPTH = 2
DEC_BATCH = 32
DEC_SEQ = 16
PAST_LEN = 2048

CHUNK = 64
MIX_WIDTH = D_MODEL
ATTN_WIDTH = MIX_WIDTH // 2
CONV_CH = MIX_WIDTH - ATTN_WIDTH
HEAD_DIM = 64
N_HEADS = ATTN_WIDTH // HEAD_DIM
N_KV_HEADS = 4
GROUP = N_HEADS // N_KV_HEADS
KV_WIDTH = N_KV_HEADS * HEAD_DIM
WINDOW = 128
BAND_CHUNKS = WINDOW // CHUNK
CONV_WIDTH = 31
CONV_STATE = CONV_WIDTH - 1
NUM_BUCKETS = 32
MAX_DISTANCE = 128
PEER_HEADS = 8
PEER_NKEYS = 128
PEER_EXPERTS = PEER_NKEYS * PEER_NKEYS
PEER_DKEY = 256
PEER_HALF = PEER_DKEY // 2
PEER_TOPK = 16
PEER_BLOCK = 256
IN_WIDTH = ATTN_WIDTH + 2 * KV_WIDTH + 2 * CONV_CH
EPS = 1e-6
ATTN_SCALE = 1.0 / math.sqrt(HEAD_DIM)
NEG_INF = -1e30

kernel_name = 'hybrid_swa_sink_conformer_peer_stream_step'


def rms_norm(x, g):
    xf = x.astype(jnp.float32)
    y = xf * lax.rsqrt(jnp.mean(xf * xf, axis=-1, keepdims=True) + EPS)
    return (y * g.astype(jnp.float32)).astype(x.dtype)


def layer_norm(x, g, b):
    xf = x.astype(jnp.float32)
    mu = jnp.mean(xf, axis=-1, keepdims=True)
    xc = xf - mu
    y = xc * lax.rsqrt(jnp.mean(xc * xc, axis=-1, keepdims=True) + EPS)
    return (y * g.astype(jnp.float32) + b.astype(jnp.float32)).astype(x.dtype)


def rel_bucket(rel):
    nb = NUM_BUCKETS // 2
    max_exact = nb // 2
    n = jnp.abs(rel)
    large = max_exact + (jnp.log(jnp.maximum(n, 1).astype(jnp.float32) / max_exact)
                         / math.log(MAX_DISTANCE / max_exact) * (nb - max_exact)).astype(jnp.int32)
    large = jnp.minimum(large, nb - 1)
    return jnp.where(rel > 0, nb, 0) + jnp.where(n < max_exact, n, large)


def relative_bias(rel, table):
    b = table[rel_bucket(rel)].astype(jnp.float32)
    return jnp.transpose(b, (2, 0, 1)).reshape(N_KV_HEADS, GROUP, *rel.shape)


def chunk_visible(qpos, kpos):
    qc = qpos // CHUNK
    return (kpos // CHUNK <= qc) & (kpos >= qc * CHUNK - WINDOW) & (kpos >= 0)


def sink_attention(q, k, v, bias, mask, sink):
    s = jnp.einsum('...qhgd,...khd->...hgqk', q, k).astype(jnp.float32) * ATTN_SCALE + bias
    s = jnp.where(mask, s, NEG_INF)
    sk = sink.astype(jnp.float32).reshape(N_KV_HEADS, GROUP, 1, 1)
    m = jnp.maximum(jnp.max(s, axis=-1, keepdims=True), sk)
    p = jnp.exp(s - m)
    p = p / (jnp.sum(p, axis=-1, keepdims=True) + jnp.exp(sk - m))
    return jnp.einsum('...hgqk,...khd->...qhgd', p.astype(v.dtype), v)


def attn_prompt(q, k, v, rel_bias, sink):
    B, S = q.shape[:2]
    n_chunks = S // CHUNK
    band = (BAND_CHUNKS + 1) * CHUNK
    q = q.reshape(B, n_chunks, CHUNK, N_KV_HEADS, GROUP, HEAD_DIM)
    pad = ((0, 0), (WINDOW, 0), (0, 0), (0, 0))
    kc = jnp.pad(k, pad).reshape(B, n_chunks + BAND_CHUNKS, CHUNK, N_KV_HEADS, HEAD_DIM)
    vc = jnp.pad(v, pad).reshape(B, n_chunks + BAND_CHUNKS, CHUNK, N_KV_HEADS, HEAD_DIM)
    kb = jnp.concatenate([kc[:, i:i + n_chunks] for i in range(BAND_CHUNKS + 1)], axis=2)
    vb = jnp.concatenate([vc[:, i:i + n_chunks] for i in range(BAND_CHUNKS + 1)], axis=2)
    qpos = jnp.arange(S, dtype=jnp.int32).reshape(n_chunks, CHUNK)
    kpos = (jnp.arange(n_chunks, dtype=jnp.int32)[:, None] * CHUNK - WINDOW
            + jnp.arange(band, dtype=jnp.int32)[None, :])
    mask = chunk_visible(qpos[:, :, None], kpos[:, None, :])
    bias = relative_bias(kpos[0][None, :] - qpos[0][:, None], rel_bias)
    o = sink_attention(q, kb, vb, bias, mask[:, None, None], sink)
    return o.reshape(B, S, ATTN_WIDTH)


def attn_sample(q, k_new, v_new, k_cache, v_cache, rel_bias, sink):
    Bd, T = q.shape[:2]
    q = q.reshape(Bd, T, N_KV_HEADS, GROUP, HEAD_DIM)
    kk = jnp.concatenate([k_cache.astype(k_new.dtype), k_new], axis=1)
    vv = jnp.concatenate([v_cache.astype(v_new.dtype), v_new], axis=1)
    qpos = PAST_LEN + jnp.arange(T, dtype=jnp.int32)
    kpos = jnp.concatenate([PAST_LEN - WINDOW + jnp.arange(WINDOW, dtype=jnp.int32), qpos])
    mask = chunk_visible(qpos[:, None], kpos[None, :])
    bias = relative_bias(kpos[None, :] - qpos[:, None], rel_bias)
    o = sink_attention(q, kk, vv, bias, mask, sink)
    return o.reshape(Bd, T, ATTN_WIDTH)


def split_projection(z):
    o1 = ATTN_WIDTH
    o2 = o1 + KV_WIDTH
    o3 = o2 + KV_WIDTH
    o4 = o3 + CONV_CH
    return z[..., :o1], z[..., o1:o2], z[..., o2:o3], z[..., o3:o4], z[..., o4:]


def depthwise_causal_conv(xp, w, b):
    y = lax.conv_general_dilated(xp, w[:, None, :].astype(xp.dtype), window_strides=(1,), padding='VALID',
                                 dimension_numbers=('NWC', 'WIO', 'NWC'), feature_group_count=CONV_CH)
    return y + b


def conv_tail(y, ln_g, ln_b):
    return jax.nn.silu(layer_norm(y, ln_g, ln_b))


def merge_groups(attn_o, conv_o, g_a, g_c, w_out):
    return jnp.concatenate([rms_norm(attn_o, g_a), rms_norm(conv_o, g_c)], axis=-1) @ w_out


def peer(xn, wq, subkeys, u, v):
    shp = xn.shape
    xt = xn.reshape(-1, D_MODEL)
    n = xt.shape[0]
    nblk = -(-n // PEER_BLOCK)
    xt = jnp.pad(xt, ((0, nblk * PEER_BLOCK - n), (0, 0))).reshape(nblk, PEER_BLOCK, D_MODEL)

    def block(xb):
        qh = (xb @ wq).reshape(PEER_BLOCK, PEER_HEADS, 2, PEER_HALF)
        s = jnp.einsum('thpd,pnd->thpn', qh, subkeys).astype(jnp.float32)
        sv, si = lax.top_k(s, PEER_TOPK)
        cand = sv[:, :, 0, :, None] + sv[:, :, 1, None, :]
        cv, ci = lax.top_k(cand.reshape(PEER_BLOCK, PEER_HEADS, PEER_TOPK * PEER_TOPK), PEER_TOPK)
        ia = jnp.take_along_axis(si[:, :, 0], ci // PEER_TOPK, axis=-1)
        ib = jnp.take_along_axis(si[:, :, 1], ci % PEER_TOPK, axis=-1)
        idx = ia * PEER_NKEYS + ib
        g = jax.nn.softmax(cv, axis=-1)
        ue = jnp.take(u, idx, axis=0)
        h = jax.nn.gelu(jnp.einsum('thkd,td->thk', ue, xb).astype(jnp.float32))
        ve = jnp.take(v, idx, axis=0)
        return jnp.einsum('thk,thkd->td', (g * h).astype(xb.dtype), ve)

    y = lax.map(block, xt).reshape(-1, D_MODEL)[:n]
    return y.reshape(shp)


def setup_inputs(seed: int = 0) -> dict:
    key = jax.random.key(seed)
    ks = jax.random.split(key, 24)
    f32 = jnp.float32
    nrm = lambda k, shape, scale: jax.random.normal(k, shape, f32) * scale
    return {
        'x_prompt': nrm(ks[0], (BATCH, SEQ, D_MODEL), 1.0),
        'x_sample': nrm(ks[1], (DEC_BATCH, DEC_SEQ, D_MODEL), 1.0),
        'cache_k': nrm(ks[2], (DEPTH, DEC_BATCH, WINDOW, N_KV_HEADS, HEAD_DIM), 1.0),
        'cache_v': nrm(ks[3], (DEPTH, DEC_BATCH, WINDOW, N_KV_HEADS, HEAD_DIM), 1.0),
        'state_conv': nrm(ks[4], (DEPTH, DEC_BATCH, CONV_STATE, CONV_CH), 0.5),
        'rel_bias': nrm(ks[5], (NUM_BUCKETS, N_HEADS), 0.5),
        'norm1_g': 1.0 + nrm(ks[6], (DEPTH, D_MODEL), 0.01),
        'w_in': nrm(ks[7], (DEPTH, D_MODEL, IN_WIDTH), D_MODEL ** -0.5),
        'attn_sink': nrm(ks[8], (DEPTH, N_HEADS), 0.5),
        'conv_w': nrm(ks[9], (DEPTH, CONV_WIDTH, CONV_CH), CONV_WIDTH ** -0.5),
        'conv_b': nrm(ks[10], (DEPTH, CONV_CH), 0.01),
        'conv_ln_g': 1.0 + nrm(ks[11], (DEPTH, CONV_CH), 0.01),
        'conv_ln_b': nrm(ks[12], (DEPTH, CONV_CH), 0.01),
        'out_norm_attn_g': 1.0 + nrm(ks[13], (DEPTH, ATTN_WIDTH), 0.01),
        'out_norm_conv_g': 1.0 + nrm(ks[14], (DEPTH, CONV_CH), 0.01),
        'w_out': nrm(ks[15], (DEPTH, MIX_WIDTH, D_MODEL), MIX_WIDTH ** -0.5),
        'norm2_g': 1.0 + nrm(ks[16], (DEPTH, D_MODEL), 0.01),
        'peer_wq': nrm(ks[17], (DEPTH, D_MODEL, PEER_HEADS * PEER_DKEY), D_MODEL ** -0.5),
        'peer_subkeys': nrm(ks[18], (DEPTH, 2, PEER_NKEYS, PEER_HALF), PEER_HALF ** -0.5),
        'peer_u': nrm(ks[19], (DEPTH, PEER_EXPERTS, D_MODEL), D_MODEL ** -0.5),
        'peer_v': nrm(ks[20], (DEPTH, PEER_EXPERTS, D_MODEL), 0.5),
        'final_norm_g': 1.0 + nrm(ks[21], (D_MODEL,), 0.01),
    }


def reference(x_prompt, x_sample, cache_k, cache_v, state_conv, rel_bias, norm1_g, w_in, attn_sink,
              conv_w, conv_b, conv_ln_g, conv_ln_b, out_norm_attn_g, out_norm_conv_g, w_out, norm2_g,
              peer_wq, peer_subkeys, peer_u, peer_v, final_norm_g):
    B, S = x_prompt.shape[:2]
    Bd, T = x_sample.shape[:2]
    hp, hs = x_prompt, x_sample
    kp_l, vp_l, cp_l, ks_l, vs_l, cs_l = [], [], [], [], [], []
    for l in range(DEPTH):
        qp, kp, vp, ap, gp = split_projection(rms_norm(hp, norm1_g[l]) @ w_in[l])
        kp = kp.reshape(B, S, N_KV_HEADS, HEAD_DIM)
        vp = vp.reshape(B, S, N_KV_HEADS, HEAD_DIM)
        att_p = attn_prompt(qp, kp, vp, rel_bias, attn_sink[l])
        glu_p = ap * jax.nn.sigmoid(gp)
        conv_p = conv_tail(depthwise_causal_conv(jnp.pad(glu_p, ((0, 0), (CONV_STATE, 0), (0, 0))),
                                                 conv_w[l], conv_b[l]), conv_ln_g[l], conv_ln_b[l])
        hp = hp + merge_groups(att_p, conv_p, out_norm_attn_g[l], out_norm_conv_g[l], w_out[l])
        hp = hp + peer(rms_norm(hp, norm2_g[l]), peer_wq[l], peer_subkeys[l], peer_u[l], peer_v[l])
        kp_l.append(kp[:, -WINDOW:])
        vp_l.append(vp[:, -WINDOW:])
        cp_l.append(glu_p[:, -CONV_STATE:])
        qs, kn, vn, as_, gs = split_projection(rms_norm(hs, norm1_g[l]) @ w_in[l])
        kn = kn.reshape(Bd, T, N_KV_HEADS, HEAD_DIM)
        vn = vn.reshape(Bd, T, N_KV_HEADS, HEAD_DIM)
        att_s = attn_sample(qs, kn, vn, cache_k[l], cache_v[l], rel_bias, attn_sink[l])
        glu_s = as_ * jax.nn.sigmoid(gs)
        conv_in = jnp.concatenate([state_conv[l].astype(glu_s.dtype), glu_s], axis=1)
        conv_s = conv_tail(depthwise_causal_conv(conv_in, conv_w[l], conv_b[l]), conv_ln_g[l], conv_ln_b[l])
        hs = hs + merge_groups(att_s, conv_s, out_norm_attn_g[l], out_norm_conv_g[l], w_out[l])
        hs = hs + peer(rms_norm(hs, norm2_g[l]), peer_wq[l], peer_subkeys[l], peer_u[l], peer_v[l])
        ks_l.append(kn)
        vs_l.append(vn)
        cs_l.append(conv_in[:, -CONV_STATE:])
    y_prompt = rms_norm(hp, final_norm_g)
    y_sample = rms_norm(hs, final_norm_g)
    new_k_prompt = jnp.stack(kp_l)
    new_v_prompt = jnp.stack(vp_l)
    new_conv_prompt = jnp.stack(cp_l)
    new_k_sample = jnp.stack(ks_l)
    new_v_sample = jnp.stack(vs_l)
    new_conv_sample = jnp.stack(cs_l)
    return (y_prompt, y_sample, new_k_prompt, new_v_prompt, new_conv_prompt, new_k_sample, new_v_sample, new_conv_sample)
```

```python
import functools
import math

import jax
import jax.numpy as jnp
from jax import lax
from jax.experimental import pallas as pl
from jax.experimental.pallas import tpu as pltpu

F32 = jnp.float32
I32 = jnp.int32

D_MODEL = 2048
CHUNK = 64
ATTN_WIDTH = 1024
CONV_CH = 1024
HEAD_DIM = 64
N_HEADS = 16
N_KV_HEADS = 4
GROUP = N_HEADS // N_KV_HEADS
KV_WIDTH = N_KV_HEADS * HEAD_DIM
WINDOW = 128
BAND = WINDOW + CHUNK
CONV_WIDTH = 31
CONV_STATE = CONV_WIDTH - 1
CONV_HALO = 32
NUM_BUCKETS = 32
MAX_DISTANCE = 128
PEER_HEADS = 8
PEER_NKEYS = 128
PEER_HALF = 128
PEER_TOPK = 16
PEER_PICKS = PEER_HEADS * PEER_TOPK
IN_WIDTH = ATTN_WIDTH + 2 * KV_WIDTH + 2 * CONV_CH
EPS = 1e-6
ATTN_SCALE = 1.0 / math.sqrt(HEAD_DIM)
NEG_INF = -1e30

Q_OFF = 0
A_OFF = ATTN_WIDTH
G_OFF = A_OFF + CONV_CH
K_OFF = G_OFF + CONV_CH
V_OFF = K_OFF + KV_WIDTH

VMEM_LIMIT = 56 * 1024 * 1024


def _params(sem, vmem=VMEM_LIMIT):
    return pltpu.CompilerParams(dimension_semantics=sem, vmem_limit_bytes=vmem)


def _rms(x, g):
    return x * lax.rsqrt(jnp.mean(x * x, axis=-1, keepdims=True) + EPS) * g


def _rms_matmul_kernel(x_ref, g_ref, w_ref, o_ref, xn_ref):
    @pl.when(pl.program_id(1) == 0)
    def _():
        xn_ref[...] = _rms(x_ref[...], g_ref[...])

    o_ref[...] = jnp.dot(xn_ref[...], w_ref[...], preferred_element_type=F32)


def rms_matmul(x, g, w, *, tm=512, tn=512):
    n, d = x.shape
    m = w.shape[1]
    return pl.pallas_call(
        _rms_matmul_kernel,
        grid=(n // tm, m // tn),
        in_specs=[pl.BlockSpec((tm, d), lambda i, j: (i, 0)),
                  pl.BlockSpec((1, d), lambda i, j: (0, 0)),
                  pl.BlockSpec((d, tn), lambda i, j: (0, j))],
        out_specs=pl.BlockSpec((tm, tn), lambda i, j: (i, j)),
        out_shape=jax.ShapeDtypeStruct((n, m), F32),
        scratch_shapes=[pltpu.VMEM((tm, d), F32)],
        compiler_params=_params(("parallel", "arbitrary")),
        name="rms_matmul",
    )(x, g.reshape(1, d), w)


def _attn_heads(q, kk, vv, bias_ref, sink_ref, mask):
    outs = []
    for h in range(N_HEADS):
        kvh = h // GROUP
        qh = q[:, h * HEAD_DIM:(h + 1) * HEAD_DIM]
        kh = kk[:, kvh * HEAD_DIM:(kvh + 1) * HEAD_DIM]
        vh = vv[:, kvh * HEAD_DIM:(kvh + 1) * HEAD_DIM]
        s = lax.dot_general(qh, kh, (((1,), (1,)), ((), ())), preferred_element_type=F32)
        s = s * ATTN_SCALE + bias_ref[h]
        if mask is not None:
            s = jnp.where(mask, s, NEG_INF)
        sk = sink_ref[h]
        m = jnp.maximum(jnp.max(s, axis=-1, keepdims=True), sk)
        p = jnp.exp(s - m)
        p = p / (jnp.sum(p, axis=-1, keepdims=True) + jnp.exp(sk - m))
        outs.append(jnp.dot(p, vh, preferred_element_type=F32))
    return jnp.concatenate(outs, axis=1)


def _attn_prompt_kernel(q_ref, k0_ref, k1_ref, k2_ref, v0_ref, v1_ref, v2_ref, bias_ref, sink_ref, o_ref):
    c = pl.program_id(1)
    kk = jnp.concatenate([k0_ref[...], k1_ref[...], k2_ref[...]], axis=0)
    vv = jnp.concatenate([v0_ref[...], v1_ref[...], v2_ref[...]], axis=0)
    col = lax.broadcasted_iota(I32, (CHUNK, BAND), 1)
    mask = col + c * CHUNK - WINDOW >= 0
    o_ref[...] = _attn_heads(q_ref[...], kk, vv, bias_ref, sink_ref, mask)


def attn_prompt(z, bias, sink, batch, seq):
    n_chunks = seq // CHUNK
    kb, vb = K_OFF // KV_WIDTH, V_OFF // KV_WIDTH

    def band_spec(back, colblock):
        return pl.BlockSpec((CHUNK, KV_WIDTH),
                            lambda b, c: (b * n_chunks + jnp.maximum(c - back, 0), colblock))

    return pl.pallas_call(
        _attn_prompt_kernel,
        grid=(batch, n_chunks),
        in_specs=[pl.BlockSpec((CHUNK, ATTN_WIDTH), lambda b, c: (b * n_chunks + c, 0)),
                  band_spec(2, kb), band_spec(1, kb), band_spec(0, kb),
                  band_spec(2, vb), band_spec(1, vb), band_spec(0, vb),
                  pl.BlockSpec((N_HEADS, CHUNK, BAND), lambda b, c: (0, 0, 0)),
                  pl.BlockSpec(memory_space=pltpu.SMEM)],
        out_specs=pl.BlockSpec((CHUNK, ATTN_WIDTH), lambda b, c: (b * n_chunks + c, 0)),
        out_shape=jax.ShapeDtypeStruct((batch * seq, ATTN_WIDTH), F32),
        compiler_params=_params(("parallel", "arbitrary")),
        name="attn_prompt",
    )(z, z, z, z, z, z, z, bias, sink)


def _attn_sample_kernel(q_ref, kn_ref, vn_ref, kc_ref, vc_ref, bias_ref, sink_ref, o_ref):
    kk = jnp.concatenate([kc_ref[0], kn_ref[...]], axis=0)
    vv = jnp.concatenate([vc_ref[0], vn_ref[...]], axis=0)
    o_ref[...] = _attn_heads(q_ref[...], kk, vv, bias_ref, sink_ref, None)


def attn_sample(z, cache_k, cache_v, bias, sink, row0, dec_batch, dec_seq):
    r0 = row0 // dec_seq
    kb, vb = K_OFF // KV_WIDTH, V_OFF // KV_WIDTH
    return pl.pallas_call(
        _attn_sample_kernel,
        grid=(dec_batch,),
        in_specs=[pl.BlockSpec((dec_seq, ATTN_WIDTH), lambda b: (r0 + b, 0)),
                  pl.BlockSpec((dec_seq, KV_WIDTH), lambda b: (r0 + b, kb)),
                  pl.BlockSpec((dec_seq, KV_WIDTH), lambda b: (r0 + b, vb)),
                  pl.BlockSpec((1, WINDOW, KV_WIDTH), lambda b: (b, 0, 0)),
                  pl.BlockSpec((1, WINDOW, KV_WIDTH), lambda b: (b, 0, 0)),
                  pl.BlockSpec((N_HEADS, dec_seq, WINDOW + dec_seq), lambda b: (0, 0, 0)),
                  pl.BlockSpec(memory_space=pltpu.SMEM)],
        out_specs=pl.BlockSpec((dec_seq, ATTN_WIDTH), lambda b: (b, 0)),
        out_shape=jax.ShapeDtypeStruct((dec_batch * dec_seq, ATTN_WIDTH), F32),
        compiler_params=_params(("parallel",)),
        name="attn_sample",
    )(z, z, z, cache_k, cache_v, bias, sink)


CONV_ROWS = 32


def _conv_kernel(pa_ref, pg_ref, a_ref, g_ref, w_ref, cb_ref, lg_ref, lb_ref, o_ref, tail_ref, ext_ref,
                 *, rows, prev_is_glu):
    if prev_is_glu:
        prev = pa_ref[0]
    else:
        prev = pa_ref[...] * jax.nn.sigmoid(pg_ref[...])
        prev = jnp.where(pl.program_id(1) == 0, 0.0, prev)
    ext_ref[0:CONV_HALO, :] = prev
    ext_ref[CONV_HALO:CONV_HALO + rows, :] = a_ref[...] * jax.nn.sigmoid(g_ref[...])
    step = min(CONV_ROWS, rows)
    for r0 in range(0, rows, step):
        acc = jnp.zeros((step, CONV_CH), F32)
        for j in range(CONV_WIDTH):
            off = r0 + j + CONV_HALO - CONV_STATE
            acc = acc + w_ref[j:j + 1, :] * ext_ref[off:off + step, :]
        y = acc + cb_ref[...]
        mu = jnp.mean(y, axis=-1, keepdims=True)
        yc = y - mu
        yn = yc * lax.rsqrt(jnp.mean(yc * yc, axis=-1, keepdims=True) + EPS) * lg_ref[...] + lb_ref[...]
        o_ref[r0:r0 + step, :] = yn * jax.nn.sigmoid(yn)
    tail_ref[0] = ext_ref[rows:rows + CONV_HALO, :]


def _conv_call(kernel, grid, in_specs, out_specs, out_shape, rows, sem, name, args):
    return pl.pallas_call(
        kernel, grid=grid, in_specs=in_specs, out_specs=out_specs, out_shape=out_shape,
        scratch_shapes=[pltpu.VMEM((CONV_HALO + rows, CONV_CH), F32)],
        compiler_params=_params(sem), name=name,
    )(*args)


def _conv_param_specs(nd):
    zero = (lambda b, t: (0, 0)) if nd == 2 else (lambda b: (0, 0))
    return [pl.BlockSpec((CONV_WIDTH, CONV_CH), zero)] + [pl.BlockSpec((1, CONV_CH), zero)] * 3


def conv_prompt(z, w, cb, lg, lb, batch, seq, *, rows=256):
    tiles = seq // rows
    halo_per_tile = rows // CONV_HALO
    ab, gb = A_OFF // CONV_CH, G_OFF // CONV_CH

    def prev_spec(colblock):
        return pl.BlockSpec(
            (CONV_HALO, CONV_CH),
            lambda b, t: (jnp.maximum((b * tiles + t) * halo_per_tile - 1, 0), colblock))

    def cur_spec(colblock):
        return pl.BlockSpec((rows, CONV_CH), lambda b, t: (b * tiles + t, colblock))

    return _conv_call(
        functools.partial(_conv_kernel, rows=rows, prev_is_glu=False),
        (batch, tiles),
        [prev_spec(ab), prev_spec(gb), cur_spec(ab), cur_spec(gb)] + _conv_param_specs(2),
        [pl.BlockSpec((rows, CONV_CH), lambda b, t: (b * tiles + t, 0)),
         pl.BlockSpec((1, CONV_HALO, CONV_CH), lambda b, t: (b, 0, 0))],
        [jax.ShapeDtypeStruct((batch * seq, CONV_CH), F32),
         jax.ShapeDtypeStruct((batch, CONV_HALO, CONV_CH), F32)],
        rows, ("parallel", "arbitrary"), "conv_prompt",
        (z, z, z, z, w, cb.reshape(1, -1), lg.reshape(1, -1), lb.reshape(1, -1)))


def conv_sample(z, state, w, cb, lg, lb, row0, dec_batch, dec_seq):
    r0 = row0 // dec_seq
    ab, gb = A_OFF // CONV_CH, G_OFF // CONV_CH
    st_spec = pl.BlockSpec((1, CONV_HALO, CONV_CH), lambda b: (b, 0, 0))
    return _conv_call(
        functools.partial(_conv_kernel, rows=dec_seq, prev_is_glu=True),
        (dec_batch,),
        [st_spec, st_spec,
         pl.BlockSpec((dec_seq, CONV_CH), lambda b: (r0 + b, ab)),
         pl.BlockSpec((dec_seq, CONV_CH), lambda b: (r0 + b, gb))] + _conv_param_specs(1),
        [pl.BlockSpec((dec_seq, CONV_CH), lambda b: (b, 0)),
         pl.BlockSpec((1, CONV_HALO, CONV_CH), lambda b: (b, 0, 0))],
        [jax.ShapeDtypeStruct((dec_batch * dec_seq, CONV_CH), F32),
         jax.ShapeDtypeStruct((dec_batch, CONV_HALO, CONV_CH), F32)],
        dec_seq, ("parallel",), "conv_sample",
        (state, state, z, z, w, cb.reshape(1, -1), lg.reshape(1, -1), lb.reshape(1, -1)))


def _merge_kernel(att_ref, conv_ref, ga_ref, gc_ref, wa_ref, wc_ref, res_ref, o_ref):
    an = _rms(att_ref[...], ga_ref[...])
    cn = _rms(conv_ref[...], gc_ref[...])
    o_ref[...] = (res_ref[...] + jnp.dot(an, wa_ref[...], preferred_element_type=F32)
                  + jnp.dot(cn, wc_ref[...], preferred_element_type=F32))


def merge(att, conv, ga, gc, w_out, res, *, tm=512, tn=512):
    n = att.shape[0]
    d = w_out.shape[1]
    return pl.pallas_call(
        _merge_kernel,
        grid=(n // tm, d // tn),
        in_specs=[pl.BlockSpec((tm, ATTN_WIDTH), lambda i, j: (i, 0)),
                  pl.BlockSpec((tm, CONV_CH), lambda i, j: (i, 0)),
                  pl.BlockSpec((1, ATTN_WIDTH), lambda i, j: (0, 0)),
                  pl.BlockSpec((1, CONV_CH), lambda i, j: (0, 0)),
                  pl.BlockSpec((ATTN_WIDTH, tn), lambda i, j: (0, j)),
                  pl.BlockSpec((CONV_CH, tn), lambda i, j: (1, j)),
                  pl.BlockSpec((tm, tn), lambda i, j: (i, j))],
        out_specs=pl.BlockSpec((tm, tn), lambda i, j: (i, j)),
        out_shape=jax.ShapeDtypeStruct((n, d), F32),
        compiler_params=_params(("parallel", "arbitrary")),
        name="merge",
    )(att, conv, ga.reshape(1, -1), gc.reshape(1, -1), w_out, w_out, res)


PEER_TB = 128


def _topk_rows(s, k, payload=None):
    n = s.shape[0]
    rows = lax.broadcasted_iota(I32, s.shape, 0)
    vals, picks = [], []
    for _ in range(k):
        m = jnp.max(s, axis=0, keepdims=True)
        am = jnp.min(jnp.where(s == m, rows, n), axis=0, keepdims=True)
        hit = rows == am
        vals.append(m)
        picks.append(am if payload is None else jnp.sum(jnp.where(hit, payload, 0), axis=0, keepdims=True))
        s = jnp.where(hit, -jnp.inf, s)
    return jnp.concatenate(vals, axis=0), jnp.concatenate(picks, axis=0)


def _peer_topk_kernel(q_ref, keys_ref, idx_ref, gate_ref):
    q = q_ref[...]
    dn = (((1,), (1,)), ((), ()))
    s0 = lax.dot_general(keys_ref[0], q[:, :PEER_HALF], dn, preferred_element_type=F32)
    s1 = lax.dot_general(keys_ref[1], q[:, PEER_HALF:], dn, preferred_element_type=F32)
    v0, i0 = _topk_rows(s0, PEER_TOPK)
    v1, i1 = _topk_rows(s1, PEER_TOPK)
    cand = jnp.concatenate([v0[i:i + 1, :] + v1 for i in range(PEER_TOPK)], axis=0)
    cidx = jnp.concatenate([i0[i:i + 1, :] * PEER_NKEYS + i1 for i in range(PEER_TOPK)], axis=0)
    cv, expert = _topk_rows(cand, PEER_TOPK, payload=cidx)
    e = jnp.exp(cv - cv[0:1, :])
    idx_ref[0] = expert
    gate_ref[0] = e / jnp.sum(e, axis=0, keepdims=True)


def peer_topk(q, subkeys):
    n = q.shape[0]
    nblk = n // PEER_TB
    out = jax.ShapeDtypeStruct((nblk, PEER_PICKS, PEER_TB), I32)
    return pl.pallas_call(
        _peer_topk_kernel,
        grid=(nblk, PEER_HEADS),
        in_specs=[pl.BlockSpec((PEER_TB, 2 * PEER_HALF), lambda i, h: (i, h)),
                  pl.BlockSpec((2, PEER_NKEYS, PEER_HALF), lambda i, h: (0, 0, 0))],
        out_specs=[pl.BlockSpec((1, PEER_TOPK, PEER_TB), lambda i, h: (i, h, 0))] * 2,
        out_shape=[out, jax.ShapeDtypeStruct(out.shape, F32)],
        compiler_params=_params(("parallel", "arbitrary")),
        name="peer_topk",
    )(q, subkeys)


def _gelu_tanh(x):
    return 0.5 * x * (1.0 + jnp.tanh(math.sqrt(2.0 / math.pi) * (x + 0.044715 * x * x * x)))


def _peer_gather_kernel(idx_hbm, gate_ref, h_ref, g2_ref, gf_ref, uv_hbm, o_ref,
                        idx_ref, buf0, buf1, sems, *, final):
    blk = pl.program_id(0)
    idx_copy = pltpu.make_async_copy(idx_hbm.at[blk], idx_ref, sems.at[2])
    idx_copy.start()
    idx_copy.wait()

    def start_rows(t, buf, sem):
        base = t * PEER_PICKS
        for k in range(PEER_PICKS):
            e = idx_ref[base + k]
            pltpu.make_async_copy(uv_hbm.at[pl.ds(e, 1)], buf.at[pl.ds(k, 1)], sem).start()

    def wait_rows(buf, sem):
        pltpu.make_async_copy(uv_hbm.at[pl.ds(0, PEER_PICKS)], buf, sem).wait()

    lane = lax.broadcasted_iota(I32, (PEER_PICKS, PEER_TB), 1)

    def compute(t, buf):
        hrow = h_ref[pl.ds(t, 1), :]
        xn = _rms(hrow, g2_ref[...])
        act = _gelu_tanh(jnp.sum(buf[:, :D_MODEL] * xn, axis=1, keepdims=True))
        gate = jnp.sum(jnp.where(lane == t, gate_ref[0], 0.0), axis=1, keepdims=True)
        y = hrow + jnp.sum((gate * act) * buf[:, D_MODEL:], axis=0, keepdims=True)
        if final:
            y = _rms(y, gf_ref[...])
        o_ref[pl.ds(t, 1), :] = y

    start_rows(0, buf0, sems.at[0])

    def pair(j, carry):
        t = 2 * j
        wait_rows(buf0, sems.at[0])
        start_rows(t + 1, buf1, sems.at[1])
        compute(t, buf0)
        wait_rows(buf1, sems.at[1])
        start_rows(jnp.minimum(t + 2, PEER_TB - 1), buf0, sems.at[0])
        compute(t + 1, buf1)
        return carry

    lax.fori_loop(0, PEER_TB // 2, pair, 0)
    wait_rows(buf0, sems.at[0])


def peer_gather(idx, gate, h, g2, gf, uv, *, final):
    n, d = h.shape
    nblk = n // PEER_TB
    return pl.pallas_call(
        functools.partial(_peer_gather_kernel, final=final),
        grid=(nblk,),
        in_specs=[pl.BlockSpec(memory_space=pl.ANY),
                  pl.BlockSpec((1, PEER_PICKS, PEER_TB), lambda i: (i, 0, 0)),
                  pl.BlockSpec((PEER_TB, d), lambda i: (i, 0)),
                  pl.BlockSpec((1, d), lambda i: (0, 0)),
                  pl.BlockSpec((1, d), lambda i: (0, 0)),
                  pl.BlockSpec(memory_space=pl.ANY)],
        out_specs=pl.BlockSpec((PEER_TB, d), lambda i: (i, 0)),
        out_shape=jax.ShapeDtypeStruct((n, d), F32),
        scratch_shapes=[pltpu.SMEM((PEER_TB * PEER_PICKS,), I32),
                        pltpu.VMEM((PEER_PICKS, 2 * d), F32),
                        pltpu.VMEM((PEER_PICKS, 2 * d), F32),
                        pltpu.SemaphoreType.DMA((3,))],
        compiler_params=_params(("arbitrary",)),
        name="peer_gather",
    )(idx, gate, h, g2.reshape(1, d), gf.reshape(1, d), uv)


def _rel_bucket(rel):
    nb = NUM_BUCKETS // 2
    max_exact = nb // 2
    n = jnp.abs(rel)
    large = max_exact + (jnp.log(jnp.maximum(n, 1).astype(F32) / max_exact)
                         / math.log(MAX_DISTANCE / max_exact) * (nb - max_exact)).astype(I32)
    large = jnp.minimum(large, nb - 1)
    return jnp.where(rel > 0, nb, 0) + jnp.where(n < max_exact, n, large)


def _bias(rel, table):
    return jnp.transpose(table[_rel_bucket(rel)].astype(F32), (2, 0, 1))


def kernel(x_prompt, x_sample, cache_k, cache_v, state_conv, rel_bias, norm1_g, w_in, attn_sink, conv_w, conv_b,
           conv_ln_g, conv_ln_b, out_norm_attn_g, out_norm_conv_g, w_out, norm2_g, peer_wq, peer_subkeys, peer_u,
           peer_v, final_norm_g):
    batch, seq, d = x_prompt.shape
    dec_batch, dec_seq, _ = x_sample.shape
    depth = w_in.shape[0]
    n_p, n_s = batch * seq, dec_batch * dec_seq
    n = n_p + n_s
    nblk = n // PEER_TB

    h = jnp.concatenate([x_prompt.reshape(n_p, d), x_sample.reshape(n_s, d)], axis=0)

    qi = jnp.arange(CHUNK, dtype=I32)
    bias_p = _bias((jnp.arange(BAND, dtype=I32) - WINDOW)[None, :] - qi[:, None], rel_bias)
    ti = jnp.arange(dec_seq, dtype=I32)
    kpos_s = jnp.concatenate([jnp.arange(WINDOW, dtype=I32) - WINDOW, ti])
    bias_s = _bias(kpos_s[None, :] - ti[:, None], rel_bias)

    outs = {k: [] for k in ("kp", "vp", "cp", "ks", "vs", "cs")}
    for l in range(depth):
        wl = w_in[l]
        o1 = ATTN_WIDTH
        o2 = o1 + KV_WIDTH
        o3 = o2 + KV_WIDTH
        w_perm = jnp.concatenate([wl[:, :o1], wl[:, o3:], wl[:, o1:o3]], axis=1)
        z = rms_matmul(h, norm1_g[l], w_perm)

        att_p = attn_prompt(z, bias_p, attn_sink[l], batch, seq)
        att_s = attn_sample(z, cache_k[l].reshape(dec_batch, WINDOW, KV_WIDTH),
                            cache_v[l].reshape(dec_batch, WINDOW, KV_WIDTH), bias_s, attn_sink[l],
                            n_p, dec_batch, dec_seq)
        conv_p, tail_p = conv_prompt(z, conv_w[l], conv_b[l], conv_ln_g[l], conv_ln_b[l], batch, seq)
        state = jnp.pad(state_conv[l], ((0, 0), (CONV_HALO - CONV_STATE, 0), (0, 0)))
        conv_s, tail_s = conv_sample(z, state, conv_w[l], conv_b[l], conv_ln_g[l], conv_ln_b[l],
                                     n_p, dec_batch, dec_seq)
        h = merge(jnp.concatenate([att_p, att_s], axis=0), jnp.concatenate([conv_p, conv_s], axis=0),
                  out_norm_attn_g[l], out_norm_conv_g[l], w_out[l], h)

        q = rms_matmul(h, norm2_g[l], peer_wq[l])
        idx_t, gate = peer_topk(q, peer_subkeys[l])
        idx = jnp.transpose(idx_t, (0, 2, 1)).reshape(nblk, PEER_TB * PEER_PICKS)
        uv = jnp.concatenate([peer_u[l], peer_v[l]], axis=1)
        h = peer_gather(idx, gate, h, norm2_g[l], final_norm_g, uv, final=(l == depth - 1))

        zp = z[:n_p].reshape(batch, seq, IN_WIDTH)
        zs = z[n_p:].reshape(dec_batch, dec_seq, IN_WIDTH)
        outs["kp"].append(zp[:, -WINDOW:, K_OFF:V_OFF].reshape(batch, WINDOW, N_KV_HEADS, HEAD_DIM))
        outs["vp"].append(zp[:, -WINDOW:, V_OFF:].reshape(batch, WINDOW, N_KV_HEADS, HEAD_DIM))
        outs["cp"].append(tail_p[:, CONV_HALO - CONV_STATE:])
        outs["ks"].append(zs[:, :, K_OFF:V_OFF].reshape(dec_batch, dec_seq, N_KV_HEADS, HEAD_DIM))
        outs["vs"].append(zs[:, :, V_OFF:].reshape(dec_batch, dec_seq, N_KV_HEADS, HEAD_DIM))
        outs["cs"].append(tail_s[:, CONV_HALO - CONV_STATE:])

    y_prompt = h[:n_p].reshape(batch, seq, d)
    y_sample = h[n_p:].reshape(dec_batch, dec_seq, d)
    return (y_prompt, y_sample, jnp.stack(outs["kp"]), jnp.stack(outs["vp"]), jnp.stack(outs["cp"]),
            jnp.stack(outs["ks"]), jnp.stack(outs["vs"]), jnp.stack(outs["cs"]))
```

```python
import functools
import math

import jax
import jax.numpy as jnp
from jax import lax
from jax.experimental import pallas as pl
from jax.experimental.pallas import tpu as pltpu

F32 = jnp.float32
I32 = jnp.int32

D_MODEL = 2048
CHUNK = 64
ATTN_WIDTH = 1024
CONV_CH = 1024
HEAD_DIM = 64
N_HEADS = 16
N_KV_HEADS = 4
GROUP = N_HEADS // N_KV_HEADS
KV_WIDTH = N_KV_HEADS * HEAD_DIM
WINDOW = 128
BAND = WINDOW + CHUNK
CONV_WIDTH = 31
CONV_STATE = CONV_WIDTH - 1
CONV_HALO = 32
NUM_BUCKETS = 32
MAX_DISTANCE = 128
PEER_HEADS = 8
PEER_NKEYS = 128
PEER_HALF = 128
PEER_TOPK = 16
PEER_PICKS = PEER_HEADS * PEER_TOPK
IN_WIDTH = ATTN_WIDTH + 2 * KV_WIDTH + 2 * CONV_CH
EPS = 1e-6
ATTN_SCALE = 1.0 / math.sqrt(HEAD_DIM)
NEG_INF = -1e30

Q_OFF = 0
A_OFF = ATTN_WIDTH
G_OFF = A_OFF + CONV_CH
K_OFF = G_OFF + CONV_CH
V_OFF = K_OFF + KV_WIDTH

VMEM_LIMIT = 56 * 1024 * 1024


def _params(sem, vmem=VMEM_LIMIT):
    return pltpu.CompilerParams(dimension_semantics=sem, vmem_limit_bytes=vmem)


def _rms(x, g):
    return x * lax.rsqrt(jnp.mean(x * x, axis=-1, keepdims=True) + EPS) * g


def _rms_matmul_kernel(x_ref, g_ref, w_ref, o_ref, xn_ref):
    @pl.when(pl.program_id(1) == 0)
    def _():
        xn_ref[...] = _rms(x_ref[...], g_ref[...])

    o_ref[...] = jnp.dot(xn_ref[...], w_ref[...], preferred_element_type=F32)


def rms_matmul(x, g, w, *, tm=512, tn=512):
    n, d = x.shape
    m = w.shape[1]
    return pl.pallas_call(
        _rms_matmul_kernel,
        grid=(n // tm, m // tn),
        in_specs=[pl.BlockSpec((tm, d), lambda i, j: (i, 0)),
                  pl.BlockSpec((1, d), lambda i, j: (0, 0)),
                  pl.BlockSpec((d, tn), lambda i, j: (0, j))],
        out_specs=pl.BlockSpec((tm, tn), lambda i, j: (i, j)),
        out_shape=jax.ShapeDtypeStruct((n, m), F32),
        scratch_shapes=[pltpu.VMEM((tm, d), F32)],
        compiler_params=_params(("parallel", "arbitrary")),
        name="rms_matmul",
    )(x, g.reshape(1, d), w)


def _attn_heads(q, kk, vv, bias_ref, sink_ref, mask):
    outs = []
    for h in range(N_HEADS):
        kvh = h // GROUP
        qh = q[:, h * HEAD_DIM:(h + 1) * HEAD_DIM]
        kh = kk[:, kvh * HEAD_DIM:(kvh + 1) * HEAD_DIM]
        vh = vv[:, kvh * HEAD_DIM:(kvh + 1) * HEAD_DIM]
        s = lax.dot_general(qh, kh, (((1,), (1,)), ((), ())), preferred_element_type=F32)
        s = s * ATTN_SCALE + bias_ref[h]
        if mask is not None:
            s = jnp.where(mask, s, NEG_INF)
        sk = sink_ref[h]
        m = jnp.maximum(jnp.max(s, axis=-1, keepdims=True), sk)
        p = jnp.exp(s - m)
        p = p / (jnp.sum(p, axis=-1, keepdims=True) + jnp.exp(sk - m))
        outs.append(jnp.dot(p, vh, preferred_element_type=F32))
    return jnp.concatenate(outs, axis=1)


def _attn_prompt_kernel(q_ref, k0_ref, k1_ref, k2_ref, v0_ref, v1_ref, v2_ref, bias_ref, sink_ref, o_ref):
    c = pl.program_id(1)
    kk = jnp.concatenate([k0_ref[...], k1_ref[...], k2_ref[...]], axis=0)
    vv = jnp.concatenate([v0_ref[...], v1_ref[...], v2_ref[...]], axis=0)
    col = lax.broadcasted_iota(I32, (CHUNK, BAND), 1)
    mask = col + c * CHUNK - WINDOW >= 0
    o_ref[...] = _attn_heads(q_ref[...], kk, vv, bias_ref, sink_ref, mask)


def attn_prompt(z, bias, sink, batch, seq):
    n_chunks = seq // CHUNK
    kb, vb = K_OFF // KV_WIDTH, V_OFF // KV_WIDTH

    def band_spec(back, colblock):
        return pl.BlockSpec((CHUNK, KV_WIDTH),
                            lambda b, c: (b * n_chunks + jnp.maximum(c - back, 0), colblock))

    return pl.pallas_call(
        _attn_prompt_kernel,
        grid=(batch, n_chunks),
        in_specs=[pl.BlockSpec((CHUNK, ATTN_WIDTH), lambda b, c: (b * n_chunks + c, 0)),
                  band_spec(2, kb), band_spec(1, kb), band_spec(0, kb),
                  band_spec(2, vb), band_spec(1, vb), band_spec(0, vb),
                  pl.BlockSpec((N_HEADS, CHUNK, BAND), lambda b, c: (0, 0, 0)),
                  pl.BlockSpec(memory_space=pltpu.SMEM)],
        out_specs=pl.BlockSpec((CHUNK, ATTN_WIDTH), lambda b, c: (b * n_chunks + c, 0)),
        out_shape=jax.ShapeDtypeStruct((batch * seq, ATTN_WIDTH), F32),
        compiler_params=_params(("parallel", "arbitrary")),
        name="attn_prompt",
    )(z, z, z, z, z, z, z, bias, sink)


def _attn_sample_kernel(q_ref, kn_ref, vn_ref, kc_ref, vc_ref, bias_ref, sink_ref, o_ref):
    kk = jnp.concatenate([kc_ref[0], kn_ref[...]], axis=0)
    vv = jnp.concatenate([vc_ref[0], vn_ref[...]], axis=0)
    o_ref[...] = _attn_heads(q_ref[...], kk, vv, bias_ref, sink_ref, None)


def attn_sample(z, cache_k, cache_v, bias, sink, row0, dec_batch, dec_seq):
    r0 = row0 // dec_seq
    kb, vb = K_OFF // KV_WIDTH, V_OFF // KV_WIDTH
    return pl.pallas_call(
        _attn_sample_kernel,
        grid=(dec_batch,),
        in_specs=[pl.BlockSpec((dec_seq, ATTN_WIDTH), lambda b: (r0 + b, 0)),
                  pl.BlockSpec((dec_seq, KV_WIDTH), lambda b: (r0 + b, kb)),
                  pl.BlockSpec((dec_seq, KV_WIDTH), lambda b: (r0 + b, vb)),
                  pl.BlockSpec((1, WINDOW, KV_WIDTH), lambda b: (b, 0, 0)),
                  pl.BlockSpec((1, WINDOW, KV_WIDTH), lambda b: (b, 0, 0)),
                  pl.BlockSpec((N_HEADS, dec_seq, WINDOW + dec_seq), lambda b: (0, 0, 0)),
                  pl.BlockSpec(memory_space=pltpu.SMEM)],
        out_specs=pl.BlockSpec((dec_seq, ATTN_WIDTH), lambda b: (b, 0)),
        out_shape=jax.ShapeDtypeStruct((dec_batch * dec_seq, ATTN_WIDTH), F32),
        compiler_params=_params(("parallel",)),
        name="attn_sample",
    )(z, z, z, cache_k, cache_v, bias, sink)


CONV_ROWS = 32


def _conv_kernel(pa_ref, pg_ref, a_ref, g_ref, w_ref, cb_ref, lg_ref, lb_ref, o_ref, tail_ref, ext_ref,
                 *, rows, prev_is_glu):
    if prev_is_glu:
        prev = pa_ref[0]
    else:
        prev = pa_ref[...] * jax.nn.sigmoid(pg_ref[...])
        prev = jnp.where(pl.program_id(1) == 0, 0.0, prev)
    ext_ref[0:CONV_HALO, :] = prev
    ext_ref[CONV_HALO:CONV_HALO + rows, :] = a_ref[...] * jax.nn.sigmoid(g_ref[...])
    step = min(CONV_ROWS, rows)
    for r0 in range(0, rows, step):
        acc = jnp.zeros((step, CONV_CH), F32)
        for j in range(CONV_WIDTH):
            off = r0 + j + CONV_HALO - CONV_STATE
            acc = acc + w_ref[j:j + 1, :] * ext_ref[off:off + step, :]
        y = acc + cb_ref[...]
        mu = jnp.mean(y, axis=-1, keepdims=True)
        yc = y - mu
        yn = yc * lax.rsqrt(jnp.mean(yc * yc, axis=-1, keepdims=True) + EPS) * lg_ref[...] + lb_ref[...]
        o_ref[r0:r0 + step, :] = yn * jax.nn.sigmoid(yn)
    tail_ref[0] = ext_ref[rows:rows + CONV_HALO, :]


def _conv_call(kernel, grid, in_specs, out_specs, out_shape, rows, sem, name, args):
    return pl.pallas_call(
        kernel, grid=grid, in_specs=in_specs, out_specs=out_specs, out_shape=out_shape,
        scratch_shapes=[pltpu.VMEM((CONV_HALO + rows, CONV_CH), F32)],
        compiler_params=_params(sem), name=name,
    )(*args)


def _conv_param_specs(nd):
    zero = (lambda b, t: (0, 0)) if nd == 2 else (lambda b: (0, 0))
    return [pl.BlockSpec((CONV_WIDTH, CONV_CH), zero)] + [pl.BlockSpec((1, CONV_CH), zero)] * 3


def conv_prompt(z, w, cb, lg, lb, batch, seq, *, rows=256):
    tiles = seq // rows
    halo_per_tile = rows // CONV_HALO
    ab, gb = A_OFF // CONV_CH, G_OFF // CONV_CH

    def prev_spec(colblock):
        return pl.BlockSpec(
            (CONV_HALO, CONV_CH),
            lambda b, t: (jnp.maximum((b * tiles + t) * halo_per_tile - 1, 0), colblock))

    def cur_spec(colblock):
        return pl.BlockSpec((rows, CONV_CH), lambda b, t: (b * tiles + t, colblock))

    return _conv_call(
        functools.partial(_conv_kernel, rows=rows, prev_is_glu=False),
        (batch, tiles),
        [prev_spec(ab), prev_spec(gb), cur_spec(ab), cur_spec(gb)] + _conv_param_specs(2),
        [pl.BlockSpec((rows, CONV_CH), lambda b, t: (b * tiles + t, 0)),
         pl.BlockSpec((1, CONV_HALO, CONV_CH), lambda b, t: (b, 0, 0))],
        [jax.ShapeDtypeStruct((batch * seq, CONV_CH), F32),
         jax.ShapeDtypeStruct((batch, CONV_HALO, CONV_CH), F32)],
        rows, ("parallel", "arbitrary"), "conv_prompt",
        (z, z, z, z, w, cb.reshape(1, -1), lg.reshape(1, -1), lb.reshape(1, -1)))


def conv_sample(z, state, w, cb, lg, lb, row0, dec_batch, dec_seq):
    r0 = row0 // dec_seq
    ab, gb = A_OFF // CONV_CH, G_OFF // CONV_CH
    st_spec = pl.BlockSpec((1, CONV_HALO, CONV_CH), lambda b: (b, 0, 0))
    return _conv_call(
        functools.partial(_conv_kernel, rows=dec_seq, prev_is_glu=True),
        (dec_batch,),
        [st_spec, st_spec,
         pl.BlockSpec((dec_seq, CONV_CH), lambda b: (r0 + b, ab)),
         pl.BlockSpec((dec_seq, CONV_CH), lambda b: (r0 + b, gb))] + _conv_param_specs(1),
        [pl.BlockSpec((dec_seq, CONV_CH), lambda b: (b, 0)),
         pl.BlockSpec((1, CONV_HALO, CONV_CH), lambda b: (b, 0, 0))],
        [jax.ShapeDtypeStruct((dec_batch * dec_seq, CONV_CH), F32),
         jax.ShapeDtypeStruct((dec_batch, CONV_HALO, CONV_CH), F32)],
        dec_seq, ("parallel",), "conv_sample",
        (state, state, z, z, w, cb.reshape(1, -1), lg.reshape(1, -1), lb.reshape(1, -1)))


def _merge_kernel(att_ref, conv_ref, ga_ref, gc_ref, wa_ref, wc_ref, res_ref, o_ref):
    an = _rms(att_ref[...], ga_ref[...])
    cn = _rms(conv_ref[...], gc_ref[...])
    o_ref[...] = (res_ref[...] + jnp.dot(an, wa_ref[...], preferred_element_type=F32)
                  + jnp.dot(cn, wc_ref[...], preferred_element_type=F32))


def merge(att, conv, ga, gc, w_out, res, *, tm=512, tn=512):
    n = att.shape[0]
    d = w_out.shape[1]
    return pl.pallas_call(
        _merge_kernel,
        grid=(n // tm, d // tn),
        in_specs=[pl.BlockSpec((tm, ATTN_WIDTH), lambda i, j: (i, 0)),
                  pl.BlockSpec((tm, CONV_CH), lambda i, j: (i, 0)),
                  pl.BlockSpec((1, ATTN_WIDTH), lambda i, j: (0, 0)),
                  pl.BlockSpec((1, CONV_CH), lambda i, j: (0, 0)),
                  pl.BlockSpec((ATTN_WIDTH, tn), lambda i, j: (0, j)),
                  pl.BlockSpec((CONV_CH, tn), lambda i, j: (1, j)),
                  pl.BlockSpec((tm, tn), lambda i, j: (i, j))],
        out_specs=pl.BlockSpec((tm, tn), lambda i, j: (i, j)),
        out_shape=jax.ShapeDtypeStruct((n, d), F32),
        compiler_params=_params(("parallel", "arbitrary")),
        name="merge",
    )(att, conv, ga.reshape(1, -1), gc.reshape(1, -1), w_out, w_out, res)


PEER_TB = 128


def _topk_rows(s, k, payload=None):
    n = s.shape[0]
    rows = lax.broadcasted_iota(I32, s.shape, 0)
    vals, picks = [], []
    for _ in range(k):
        m = jnp.max(s, axis=0, keepdims=True)
        am = jnp.min(jnp.where(s == m, rows, n), axis=0, keepdims=True)
        hit = rows == am
        vals.append(m)
        picks.append(am if payload is None else jnp.sum(jnp.where(hit, payload, 0), axis=0, keepdims=True))
        s = jnp.where(hit, -jnp.inf, s)
    return jnp.concatenate(vals, axis=0), jnp.concatenate(picks, axis=0)


def _peer_topk_kernel(q_ref, keys_ref, idx_ref, gate_ref):
    q = q_ref[...]
    dn = (((1,), (1,)), ((), ()))
    s0 = lax.dot_general(keys_ref[0], q[:, :PEER_HALF], dn, preferred_element_type=F32)
    s1 = lax.dot_general(keys_ref[1], q[:, PEER_HALF:], dn, preferred_element_type=F32)
    v0, i0 = _topk_rows(s0, PEER_TOPK)
    v1, i1 = _topk_rows(s1, PEER_TOPK)
    cand = jnp.concatenate([v0[i:i + 1, :] + v1 for i in range(PEER_TOPK)], axis=0)
    cidx = jnp.concatenate([i0[i:i + 1, :] * PEER_NKEYS + i1 for i in range(PEER_TOPK)], axis=0)
    cv, expert = _topk_rows(cand, PEER_TOPK, payload=cidx)
    e = jnp.exp(cv - cv[0:1, :])
    idx_ref[0] = expert
    gate_ref[0] = e / jnp.sum(e, axis=0, keepdims=True)


def peer_topk(q, subkeys):
    n = q.shape[0]
    nblk = n // PEER_TB
    out = jax.ShapeDtypeStruct((nblk, PEER_PICKS, PEER_TB), I32)
    return pl.pallas_call(
        _peer_topk_kernel,
        grid=(nblk, PEER_HEADS),
        in_specs=[pl.BlockSpec((PEER_TB, 2 * PEER_HALF), lambda i, h: (i, h)),
                  pl.BlockSpec((2, PEER_NKEYS, PEER_HALF), lambda i, h: (0, 0, 0))],
        out_specs=[pl.BlockSpec((1, PEER_TOPK, PEER_TB), lambda i, h: (i, h, 0))] * 2,
        out_shape=[out, jax.ShapeDtypeStruct(out.shape, F32)],
        compiler_params=_params(("parallel", "arbitrary")),
        name="peer_topk",
    )(q, subkeys)


def _gelu_tanh(x):
    return 0.5 * x * (1.0 + jnp.tanh(math.sqrt(2.0 / math.pi) * (x + 0.044715 * x * x * x)))


GATHER_DEPTH = 8


def _peer_gather_kernel(idx_hbm, gate_ref, h_ref, g2_ref, gf_ref, uv_hbm, o_ref, idx_ref, *scratch, final):
    bufs, sems = scratch[:GATHER_DEPTH], scratch[GATHER_DEPTH]
    blk = pl.program_id(0)
    idx_copy = pltpu.make_async_copy(idx_hbm.at[blk], idx_ref, sems.at[GATHER_DEPTH])
    idx_copy.start()
    idx_copy.wait()

    def start_rows(t, slot):
        base = t * PEER_PICKS
        for k in range(PEER_PICKS):
            e = idx_ref[base + k]
            pltpu.make_async_copy(uv_hbm.at[pl.ds(e, 1)], bufs[slot].at[pl.ds(k, 1)], sems.at[slot]).start()

    def wait_rows(slot):
        pltpu.make_async_copy(uv_hbm.at[pl.ds(0, PEER_PICKS)], bufs[slot], sems.at[slot]).wait()

    lane = lax.broadcasted_iota(I32, (PEER_PICKS, PEER_TB), 1)

    def compute(t, slot):
        buf = bufs[slot]
        hrow = h_ref[pl.ds(t, 1), :]
        xn = _rms(hrow, g2_ref[...])
        act = _gelu_tanh(jnp.sum(buf[:, :D_MODEL] * xn, axis=1, keepdims=True))
        gate = jnp.sum(jnp.where(lane == t, gate_ref[0], 0.0), axis=1, keepdims=True)
        y = hrow + jnp.sum((gate * act) * buf[:, D_MODEL:], axis=0, keepdims=True)
        if final:
            y = _rms(y, gf_ref[...])
        o_ref[pl.ds(t, 1), :] = y

    def group(j, last):
        for b in range(GATHER_DEPTH):
            t = j * GATHER_DEPTH + b
            if not last or b == 0:
                start_rows(t + GATHER_DEPTH - 1, (b + GATHER_DEPTH - 1) % GATHER_DEPTH)
            wait_rows(b)
            compute(t, b)

    for b in range(GATHER_DEPTH - 1):
        start_rows(b, b)
    groups = PEER_TB // GATHER_DEPTH

    def body(j, carry):
        group(j, False)
        return carry

    lax.fori_loop(0, groups - 1, body, 0)
    group(groups - 1, True)


def peer_gather(idx, gate, h, g2, gf, uv, *, final):
    n, d = h.shape
    nblk = n // PEER_TB
    return pl.pallas_call(
        functools.partial(_peer_gather_kernel, final=final),
        grid=(nblk,),
        in_specs=[pl.BlockSpec(memory_space=pl.ANY),
                  pl.BlockSpec((1, PEER_PICKS, PEER_TB), lambda i: (i, 0, 0)),
                  pl.BlockSpec((PEER_TB, d), lambda i: (i, 0)),
                  pl.BlockSpec((1, d), lambda i: (0, 0)),
                  pl.BlockSpec((1, d), lambda i: (0, 0)),
                  pl.BlockSpec(memory_space=pl.ANY)],
        out_specs=pl.BlockSpec((PEER_TB, d), lambda i: (i, 0)),
        out_shape=jax.ShapeDtypeStruct((n, d), F32),
        scratch_shapes=([pltpu.SMEM((PEER_TB * PEER_PICKS,), I32)]
                        + [pltpu.VMEM((PEER_PICKS, 2 * d), F32)] * GATHER_DEPTH
                        + [pltpu.SemaphoreType.DMA((GATHER_DEPTH + 1,))]),
        compiler_params=_params(("arbitrary",)),
        name="peer_gather",
    )(idx, gate, h, g2.reshape(1, d), gf.reshape(1, d), uv)


def _rel_bucket(rel):
    nb = NUM_BUCKETS // 2
    max_exact = nb // 2
    n = jnp.abs(rel)
    large = max_exact + (jnp.log(jnp.maximum(n, 1).astype(F32) / max_exact)
                         / math.log(MAX_DISTANCE / max_exact) * (nb - max_exact)).astype(I32)
    large = jnp.minimum(large, nb - 1)
    return jnp.where(rel > 0, nb, 0) + jnp.where(n < max_exact, n, large)


def _bias(rel, table):
    return jnp.transpose(table[_rel_bucket(rel)].astype(F32), (2, 0, 1))


def kernel(x_prompt, x_sample, cache_k, cache_v, state_conv, rel_bias, norm1_g, w_in, attn_sink, conv_w, conv_b,
           conv_ln_g, conv_ln_b, out_norm_attn_g, out_norm_conv_g, w_out, norm2_g, peer_wq, peer_subkeys, peer_u,
           peer_v, final_norm_g):
    batch, seq, d = x_prompt.shape
    dec_batch, dec_seq, _ = x_sample.shape
    depth = w_in.shape[0]
    n_p, n_s = batch * seq, dec_batch * dec_seq
    n = n_p + n_s
    nblk = n // PEER_TB

    h = jnp.concatenate([x_prompt.reshape(n_p, d), x_sample.reshape(n_s, d)], axis=0)

    qi = jnp.arange(CHUNK, dtype=I32)
    bias_p = _bias((jnp.arange(BAND, dtype=I32) - WINDOW)[None, :] - qi[:, None], rel_bias)
    ti = jnp.arange(dec_seq, dtype=I32)
    kpos_s = jnp.concatenate([jnp.arange(WINDOW, dtype=I32) - WINDOW, ti])
    bias_s = _bias(kpos_s[None, :] - ti[:, None], rel_bias)

    outs = {k: [] for k in ("kp", "vp", "cp", "ks", "vs", "cs")}
    for l in range(depth):
        wl = w_in[l]
        o1 = ATTN_WIDTH
        o2 = o1 + KV_WIDTH
        o3 = o2 + KV_WIDTH
        w_perm = jnp.concatenate([wl[:, :o1], wl[:, o3:], wl[:, o1:o3]], axis=1)
        z = rms_matmul(h, norm1_g[l], w_perm)

        att_p = attn_prompt(z, bias_p, attn_sink[l], batch, seq)
        att_s = attn_sample(z, cache_k[l].reshape(dec_batch, WINDOW, KV_WIDTH),
                            cache_v[l].reshape(dec_batch, WINDOW, KV_WIDTH), bias_s, attn_sink[l],
                            n_p, dec_batch, dec_seq)
        conv_p, tail_p = conv_prompt(z, conv_w[l], conv_b[l], conv_ln_g[l], conv_ln_b[l], batch, seq)
        state = jnp.pad(state_conv[l], ((0, 0), (CONV_HALO - CONV_STATE, 0), (0, 0)))
        conv_s, tail_s = conv_sample(z, state, conv_w[l], conv_b[l], conv_ln_g[l], conv_ln_b[l],
                                     n_p, dec_batch, dec_seq)
        h = merge(jnp.concatenate([att_p, att_s], axis=0), jnp.concatenate([conv_p, conv_s], axis=0),
                  out_norm_attn_g[l], out_norm_conv_g[l], w_out[l], h)

        q = rms_matmul(h, norm2_g[l], peer_wq[l])
        idx_t, gate = peer_topk(q, peer_subkeys[l])
        idx = jnp.transpose(idx_t, (0, 2, 1)).reshape(nblk, PEER_TB * PEER_PICKS)
        uv = jnp.concatenate([peer_u[l], peer_v[l]], axis=1)
        h = peer_gather(idx, gate, h, norm2_g[l], final_norm_g, uv, final=(l == depth - 1))

        zp = z[:n_p].reshape(batch, seq, IN_WIDTH)
        zs = z[n_p:].reshape(dec_batch, dec_seq, IN_WIDTH)
        outs["kp"].append(zp[:, -WINDOW:, K_OFF:V_OFF].reshape(batch, WINDOW, N_KV_HEADS, HEAD_DIM))
        outs["vp"].append(zp[:, -WINDOW:, V_OFF:].reshape(batch, WINDOW, N_KV_HEADS, HEAD_DIM))
        outs["cp"].append(tail_p[:, CONV_HALO - CONV_STATE:])
        outs["ks"].append(zs[:, :, K_OFF:V_OFF].reshape(dec_batch, dec_seq, N_KV_HEADS, HEAD_DIM))
        outs["vs"].append(zs[:, :, V_OFF:].reshape(dec_batch, dec_seq, N_KV_HEADS, HEAD_DIM))
        outs["cs"].append(tail_s[:, CONV_HALO - CONV_STATE:])

    y_prompt = h[:n_p].reshape(batch, seq, d)
    y_sample = h[n_p:].reshape(dec_batch, dec_seq, d)
    return (y_prompt, y_sample, jnp.stack(outs["kp"]), jnp.stack(outs["vp"]), jnp.stack(outs["cp"]),
            jnp.stack(outs["ks"]), jnp.stack(outs["vs"]), jnp.stack(outs["cs"]))
```

```python
import functools
import math

import jax
import jax.numpy as jnp
from jax import lax
from jax.experimental import pallas as pl
from jax.experimental.pallas import tpu as pltpu

F32 = jnp.float32
I32 = jnp.int32

D_MODEL = 2048
CHUNK = 64
ATTN_WIDTH = 1024
CONV_CH = 1024
HEAD_DIM = 64
N_HEADS = 16
N_KV_HEADS = 4
GROUP = N_HEADS // N_KV_HEADS
KV_WIDTH = N_KV_HEADS * HEAD_DIM
WINDOW = 128
BAND = WINDOW + CHUNK
CONV_WIDTH = 31
CONV_STATE = CONV_WIDTH - 1
CONV_HALO = 32
NUM_BUCKETS = 32
MAX_DISTANCE = 128
PEER_HEADS = 8
PEER_NKEYS = 128
PEER_HALF = 128
PEER_TOPK = 16
PEER_PICKS = PEER_HEADS * PEER_TOPK
IN_WIDTH = ATTN_WIDTH + 2 * KV_WIDTH + 2 * CONV_CH
EPS = 1e-6
ATTN_SCALE = 1.0 / math.sqrt(HEAD_DIM)
NEG_INF = -1e30

Q_OFF = 0
A_OFF = ATTN_WIDTH
G_OFF = A_OFF + CONV_CH
K_OFF = G_OFF + CONV_CH
V_OFF = K_OFF + KV_WIDTH

VMEM_LIMIT = 56 * 1024 * 1024


def _params(sem, vmem=VMEM_LIMIT):
    return pltpu.CompilerParams(dimension_semantics=sem, vmem_limit_bytes=vmem)


def _rms(x, g):
    return x * lax.rsqrt(jnp.mean(x * x, axis=-1, keepdims=True) + EPS) * g


def _rms_matmul_kernel(x_ref, g_ref, w_ref, o_ref, xn_ref):
    @pl.when(pl.program_id(1) == 0)
    def _():
        xn_ref[...] = _rms(x_ref[...], g_ref[...])

    o_ref[...] = jnp.dot(xn_ref[...], w_ref[...], preferred_element_type=F32)


def rms_matmul(x, g, w, *, tm=512, tn=512):
    n, d = x.shape
    m = w.shape[1]
    return pl.pallas_call(
        _rms_matmul_kernel,
        grid=(n // tm, m // tn),
        in_specs=[pl.BlockSpec((tm, d), lambda i, j: (i, 0)),
                  pl.BlockSpec((1, d), lambda i, j: (0, 0)),
                  pl.BlockSpec((d, tn), lambda i, j: (0, j))],
        out_specs=pl.BlockSpec((tm, tn), lambda i, j: (i, j)),
        out_shape=jax.ShapeDtypeStruct((n, m), F32),
        scratch_shapes=[pltpu.VMEM((tm, d), F32)],
        compiler_params=_params(("parallel", "arbitrary")),
        name="rms_matmul",
    )(x, g.reshape(1, d), w)


def _attn_heads(q, kk, vv, bias_ref, sink_ref, mask):
    outs = []
    for h in range(N_HEADS):
        kvh = h // GROUP
        qh = q[:, h * HEAD_DIM:(h + 1) * HEAD_DIM]
        kh = kk[:, kvh * HEAD_DIM:(kvh + 1) * HEAD_DIM]
        vh = vv[:, kvh * HEAD_DIM:(kvh + 1) * HEAD_DIM]
        s = lax.dot_general(qh, kh, (((1,), (1,)), ((), ())), preferred_element_type=F32)
        s = s * ATTN_SCALE + bias_ref[h]
        if mask is not None:
            s = jnp.where(mask, s, NEG_INF)
        sk = sink_ref[h]
        m = jnp.maximum(jnp.max(s, axis=-1, keepdims=True), sk)
        p = jnp.exp(s - m)
        p = p / (jnp.sum(p, axis=-1, keepdims=True) + jnp.exp(sk - m))
        outs.append(jnp.dot(p, vh, preferred_element_type=F32))
    return jnp.concatenate(outs, axis=1)


def _attn_prompt_kernel(q_ref, k0_ref, k1_ref, k2_ref, v0_ref, v1_ref, v2_ref, bias_ref, sink_ref, o_ref):
    c = pl.program_id(1)
    kk = jnp.concatenate([k0_ref[...], k1_ref[...], k2_ref[...]], axis=0)
    vv = jnp.concatenate([v0_ref[...], v1_ref[...], v2_ref[...]], axis=0)
    col = lax.broadcasted_iota(I32, (CHUNK, BAND), 1)
    mask = col + c * CHUNK - WINDOW >= 0
    o_ref[...] = _attn_heads(q_ref[...], kk, vv, bias_ref, sink_ref, mask)


def attn_prompt(z, bias, sink, batch, seq):
    n_chunks = seq // CHUNK
    kb, vb = K_OFF // KV_WIDTH, V_OFF // KV_WIDTH

    def band_spec(back, colblock):
        return pl.BlockSpec((CHUNK, KV_WIDTH),
                            lambda b, c: (b * n_chunks + jnp.maximum(c - back, 0), colblock))

    return pl.pallas_call(
        _attn_prompt_kernel,
        grid=(batch, n_chunks),
        in_specs=[pl.BlockSpec((CHUNK, ATTN_WIDTH), lambda b, c: (b * n_chunks + c, 0)),
                  band_spec(2, kb), band_spec(1, kb), band_spec(0, kb),
                  band_spec(2, vb), band_spec(1, vb), band_spec(0, vb),
                  pl.BlockSpec((N_HEADS, CHUNK, BAND), lambda b, c: (0, 0, 0)),
                  pl.BlockSpec(memory_space=pltpu.SMEM)],
        out_specs=pl.BlockSpec((CHUNK, ATTN_WIDTH), lambda b, c: (b * n_chunks + c, 0)),
        out_shape=jax.ShapeDtypeStruct((batch * seq, ATTN_WIDTH), F32),
        compiler_params=_params(("parallel", "arbitrary")),
        name="attn_prompt",
    )(z, z, z, z, z, z, z, bias, sink)


def _attn_sample_kernel(q_ref, kn_ref, vn_ref, kc_ref, vc_ref, bias_ref, sink_ref, o_ref):
    kk = jnp.concatenate([kc_ref[0], kn_ref[...]], axis=0)
    vv = jnp.concatenate([vc_ref[0], vn_ref[...]], axis=0)
    o_ref[...] = _attn_heads(q_ref[...], kk, vv, bias_ref, sink_ref, None)


def attn_sample(z, cache_k, cache_v, bias, sink, row0, dec_batch, dec_seq):
    r0 = row0 // dec_seq
    kb, vb = K_OFF // KV_WIDTH, V_OFF // KV_WIDTH
    return pl.pallas_call(
        _attn_sample_kernel,
        grid=(dec_batch,),
        in_specs=[pl.BlockSpec((dec_seq, ATTN_WIDTH), lambda b: (r0 + b, 0)),
                  pl.BlockSpec((dec_seq, KV_WIDTH), lambda b: (r0 + b, kb)),
                  pl.BlockSpec((dec_seq, KV_WIDTH), lambda b: (r0 + b, vb)),
                  pl.BlockSpec((1, WINDOW, KV_WIDTH), lambda b: (b, 0, 0)),
                  pl.BlockSpec((1, WINDOW, KV_WIDTH), lambda b: (b, 0, 0)),
                  pl.BlockSpec((N_HEADS, dec_seq, WINDOW + dec_seq), lambda b: (0, 0, 0)),
                  pl.BlockSpec(memory_space=pltpu.SMEM)],
        out_specs=pl.BlockSpec((dec_seq, ATTN_WIDTH), lambda b: (b, 0)),
        out_shape=jax.ShapeDtypeStruct((dec_batch * dec_seq, ATTN_WIDTH), F32),
        compiler_params=_params(("parallel",)),
        name="attn_sample",
    )(z, z, z, cache_k, cache_v, bias, sink)


CONV_ROWS = 32


def _conv_kernel(pa_ref, pg_ref, a_ref, g_ref, w_ref, cb_ref, lg_ref, lb_ref, o_ref, tail_ref, ext_ref,
                 *, rows, prev_is_glu):
    if prev_is_glu:
        prev = pa_ref[0]
    else:
        prev = pa_ref[...] * jax.nn.sigmoid(pg_ref[...])
        prev = jnp.where(pl.program_id(1) == 0, 0.0, prev)
    ext_ref[0:CONV_HALO, :] = prev
    ext_ref[CONV_HALO:CONV_HALO + rows, :] = a_ref[...] * jax.nn.sigmoid(g_ref[...])
    step = min(CONV_ROWS, rows)
    for r0 in range(0, rows, step):
        acc = jnp.zeros((step, CONV_CH), F32)
        for j in range(CONV_WIDTH):
            off = r0 + j + CONV_HALO - CONV_STATE
            acc = acc + w_ref[j:j + 1, :] * ext_ref[off:off + step, :]
        y = acc + cb_ref[...]
        mu = jnp.mean(y, axis=-1, keepdims=True)
        yc = y - mu
        yn = yc * lax.rsqrt(jnp.mean(yc * yc, axis=-1, keepdims=True) + EPS) * lg_ref[...] + lb_ref[...]
        o_ref[r0:r0 + step, :] = yn * jax.nn.sigmoid(yn)
    tail_ref[0] = ext_ref[rows:rows + CONV_HALO, :]


def _conv_call(kernel, grid, in_specs, out_specs, out_shape, rows, sem, name, args):
    return pl.pallas_call(
        kernel, grid=grid, in_specs=in_specs, out_specs=out_specs, out_shape=out_shape,
        scratch_shapes=[pltpu.VMEM((CONV_HALO + rows, CONV_CH), F32)],
        compiler_params=_params(sem), name=name,
    )(*args)


def _conv_param_specs(nd):
    zero = (lambda b, t: (0, 0)) if nd == 2 else (lambda b: (0, 0))
    return [pl.BlockSpec((CONV_WIDTH, CONV_CH), zero)] + [pl.BlockSpec((1, CONV_CH), zero)] * 3


def conv_prompt(z, w, cb, lg, lb, batch, seq, *, rows=256):
    tiles = seq // rows
    halo_per_tile = rows // CONV_HALO
    ab, gb = A_OFF // CONV_CH, G_OFF // CONV_CH

    def prev_spec(colblock):
        return pl.BlockSpec(
            (CONV_HALO, CONV_CH),
            lambda b, t: (jnp.maximum((b * tiles + t) * halo_per_tile - 1, 0), colblock))

    def cur_spec(colblock):
        return pl.BlockSpec((rows, CONV_CH), lambda b, t: (b * tiles + t, colblock))

    return _conv_call(
        functools.partial(_conv_kernel, rows=rows, prev_is_glu=False),
        (batch, tiles),
        [prev_spec(ab), prev_spec(gb), cur_spec(ab), cur_spec(gb)] + _conv_param_specs(2),
        [pl.BlockSpec((rows, CONV_CH), lambda b, t: (b * tiles + t, 0)),
         pl.BlockSpec((1, CONV_HALO, CONV_CH), lambda b, t: (b, 0, 0))],
        [jax.ShapeDtypeStruct((batch * seq, CONV_CH), F32),
         jax.ShapeDtypeStruct((batch, CONV_HALO, CONV_CH), F32)],
        rows, ("parallel", "arbitrary"), "conv_prompt",
        (z, z, z, z, w, cb.reshape(1, -1), lg.reshape(1, -1), lb.reshape(1, -1)))


def conv_sample(z, state, w, cb, lg, lb, row0, dec_batch, dec_seq):
    r0 = row0 // dec_seq
    ab, gb = A_OFF // CONV_CH, G_OFF // CONV_CH
    st_spec = pl.BlockSpec((1, CONV_HALO, CONV_CH), lambda b: (b, 0, 0))
    return _conv_call(
        functools.partial(_conv_kernel, rows=dec_seq, prev_is_glu=True),
        (dec_batch,),
        [st_spec, st_spec,
         pl.BlockSpec((dec_seq, CONV_CH), lambda b: (r0 + b, ab)),
         pl.BlockSpec((dec_seq, CONV_CH), lambda b: (r0 + b, gb))] + _conv_param_specs(1),
        [pl.BlockSpec((dec_seq, CONV_CH), lambda b: (b, 0)),
         pl.BlockSpec((1, CONV_HALO, CONV_CH), lambda b: (b, 0, 0))],
        [jax.ShapeDtypeStruct((dec_batch * dec_seq, CONV_CH), F32),
         jax.ShapeDtypeStruct((dec_batch, CONV_HALO, CONV_CH), F32)],
        dec_seq, ("parallel",), "conv_sample",
        (state, state, z, z, w, cb.reshape(1, -1), lg.reshape(1, -1), lb.reshape(1, -1)))


def _merge_kernel(att_ref, conv_ref, ga_ref, gc_ref, wa_ref, wc_ref, res_ref, o_ref):
    an = _rms(att_ref[...], ga_ref[...])
    cn = _rms(conv_ref[...], gc_ref[...])
    o_ref[...] = (res_ref[...] + jnp.dot(an, wa_ref[...], preferred_element_type=F32)
                  + jnp.dot(cn, wc_ref[...], preferred_element_type=F32))


def merge(att, conv, ga, gc, w_out, res, *, tm=512, tn=512):
    n = att.shape[0]
    d = w_out.shape[1]
    return pl.pallas_call(
        _merge_kernel,
        grid=(n // tm, d // tn),
        in_specs=[pl.BlockSpec((tm, ATTN_WIDTH), lambda i, j: (i, 0)),
                  pl.BlockSpec((tm, CONV_CH), lambda i, j: (i, 0)),
                  pl.BlockSpec((1, ATTN_WIDTH), lambda i, j: (0, 0)),
                  pl.BlockSpec((1, CONV_CH), lambda i, j: (0, 0)),
                  pl.BlockSpec((ATTN_WIDTH, tn), lambda i, j: (0, j)),
                  pl.BlockSpec((CONV_CH, tn), lambda i, j: (1, j)),
                  pl.BlockSpec((tm, tn), lambda i, j: (i, j))],
        out_specs=pl.BlockSpec((tm, tn), lambda i, j: (i, j)),
        out_shape=jax.ShapeDtypeStruct((n, d), F32),
        compiler_params=_params(("parallel", "arbitrary")),
        name="merge",
    )(att, conv, ga.reshape(1, -1), gc.reshape(1, -1), w_out, w_out, res)


PEER_TB = 128


def _topk_rows(s, k, payload=None):
    n = s.shape[0]
    rows = lax.broadcasted_iota(I32, s.shape, 0)
    vals, picks = [], []
    for _ in range(k):
        m = jnp.max(s, axis=0, keepdims=True)
        am = jnp.min(jnp.where(s == m, rows, n), axis=0, keepdims=True)
        hit = rows == am
        vals.append(m)
        picks.append(am if payload is None else jnp.sum(jnp.where(hit, payload, 0), axis=0, keepdims=True))
        s = jnp.where(hit, -jnp.inf, s)
    return jnp.concatenate(vals, axis=0), jnp.concatenate(picks, axis=0)


def _peer_topk_kernel(q_ref, keys_ref, idx_ref, gate_ref):
    q = q_ref[...]
    dn = (((1,), (1,)), ((), ()))
    s0 = lax.dot_general(keys_ref[0], q[:, :PEER_HALF], dn, preferred_element_type=F32)
    s1 = lax.dot_general(keys_ref[1], q[:, PEER_HALF:], dn, preferred_element_type=F32)
    v0, i0 = _topk_rows(s0, PEER_TOPK)
    v1, i1 = _topk_rows(s1, PEER_TOPK)
    cand = jnp.concatenate([v0[i:i + 1, :] + v1 for i in range(PEER_TOPK)], axis=0)
    cidx = jnp.concatenate([i0[i:i + 1, :] * PEER_NKEYS + i1 for i in range(PEER_TOPK)], axis=0)
    cv, expert = _topk_rows(cand, PEER_TOPK, payload=cidx)
    e = jnp.exp(cv - cv[0:1, :])
    idx_ref[0] = expert
    gate_ref[0] = e / jnp.sum(e, axis=0, keepdims=True)


def peer_topk(q, subkeys):
    n = q.shape[0]
    nblk = n // PEER_TB
    out = jax.ShapeDtypeStruct((nblk, PEER_PICKS, PEER_TB), I32)
    return pl.pallas_call(
        _peer_topk_kernel,
        grid=(nblk, PEER_HEADS),
        in_specs=[pl.BlockSpec((PEER_TB, 2 * PEER_HALF), lambda i, h: (i, h)),
                  pl.BlockSpec((2, PEER_NKEYS, PEER_HALF), lambda i, h: (0, 0, 0))],
        out_specs=[pl.BlockSpec((1, PEER_TOPK, PEER_TB), lambda i, h: (i, h, 0))] * 2,
        out_shape=[out, jax.ShapeDtypeStruct(out.shape, F32)],
        compiler_params=_params(("parallel", "arbitrary")),
        name="peer_topk",
    )(q, subkeys)


def _gelu_tanh(x):
    return 0.5 * x * (1.0 + jnp.tanh(math.sqrt(2.0 / math.pi) * (x + 0.044715 * x * x * x)))


GATHER_DEPTH = 8
TOKENS_PER_STEP = 2
LANES = 128


def _peer_gather_kernel(idx_hbm, gate_ref, h_ref, g2_ref, gf_ref, uv_hbm, o_ref, idx_ref, *scratch, final):
    bufs, sems = scratch[:GATHER_DEPTH], scratch[GATHER_DEPTH]
    blk = pl.program_id(0)
    idx_copy = pltpu.make_async_copy(idx_hbm.at[blk], idx_ref, sems.at[GATHER_DEPTH])
    idx_copy.start()
    idx_copy.wait()

    def start_rows(t, slot):
        base = t * PEER_PICKS
        for k in range(PEER_PICKS):
            e = idx_ref[base + k]
            pltpu.make_async_copy(uv_hbm.at[e], bufs[slot].at[pl.ds(k, 1)], sems.at[slot]).start(priority=k % 2)

    def wait_rows(slot):
        pltpu.make_async_copy(uv_hbm.at[pl.ds(0, PEER_PICKS), 0], bufs[slot], sems.at[slot]).wait()

    lane = lax.broadcasted_iota(I32, (PEER_PICKS, PEER_TB), 1)
    half = D_MODEL // 2

    def unpack(words):
        lo = lax.bitcast_convert_type(words << 16, F32)
        hi = lax.bitcast_convert_type(words & jnp.uint32(0xFFFF0000), F32)
        return lo, hi

    def compute(t, slot):
        buf = bufs[slot]
        hrow = h_ref[pl.ds(t, 1), :]
        xn = _rms(hrow, g2_ref[...])
        acc = jnp.zeros((PEER_PICKS, LANES), F32)
        for c in range(half // LANES):
            u_lo, u_hi = unpack(buf[:, c * LANES:(c + 1) * LANES])
            acc = acc + (u_lo * xn[:, c * LANES:(c + 1) * LANES]
                         + u_hi * xn[:, half + c * LANES:half + (c + 1) * LANES])
        act = _gelu_tanh(jnp.sum(acc, axis=1, keepdims=True))
        gate = jnp.sum(jnp.where(lane == t, gate_ref[0], 0.0), axis=1, keepdims=True)
        w = jnp.broadcast_to(gate * act, (PEER_PICKS, LANES))
        mix_lo, mix_hi = [], []
        for c in range(half // LANES):
            v_lo, v_hi = unpack(buf[:, half + c * LANES:half + (c + 1) * LANES])
            mix_lo.append(jnp.sum(w * v_lo, axis=0, keepdims=True))
            mix_hi.append(jnp.sum(w * v_hi, axis=0, keepdims=True))
        y = hrow + jnp.concatenate(mix_lo + mix_hi, axis=1)
        if final:
            y = _rms(y, gf_ref[...])
        o_ref[pl.ds(t, 1), :] = y

    ahead = GATHER_DEPTH - TOKENS_PER_STEP

    def group(j, last):
        for b in range(0, GATHER_DEPTH, TOKENS_PER_STEP):
            t = j * GATHER_DEPTH + b
            for i in range(TOKENS_PER_STEP):
                wait_rows(b + i)
            for i in range(TOKENS_PER_STEP):
                if not last or b + i + ahead < GATHER_DEPTH:
                    start_rows(t + i + ahead, (b + i + ahead) % GATHER_DEPTH)
            for i in range(TOKENS_PER_STEP):
                compute(t + i, b + i)

    for b in range(ahead):
        start_rows(b, b)
    groups = PEER_TB // GATHER_DEPTH

    def body(j, carry):
        group(j, False)
        return carry

    lax.fori_loop(0, groups - 1, body, 0)
    group(groups - 1, True)


def pack_expert_tables(u, v):
    def pack(x):
        bits = lax.bitcast_convert_type(x.astype(jnp.bfloat16), jnp.uint16).astype(jnp.uint32)
        half = x.shape[1] // 2
        return bits[:, :half] | (bits[:, half:] << 16)

    return jnp.concatenate([pack(u), pack(v)], axis=1)[:, None, :]


def peer_gather(idx, gate, h, g2, gf, uv, *, final):
    n, d = h.shape
    nblk = n // PEER_TB
    return pl.pallas_call(
        functools.partial(_peer_gather_kernel, final=final),
        grid=(nblk,),
        in_specs=[pl.BlockSpec(memory_space=pl.ANY),
                  pl.BlockSpec((1, PEER_PICKS, PEER_TB), lambda i: (i, 0, 0)),
                  pl.BlockSpec((PEER_TB, d), lambda i: (i, 0)),
                  pl.BlockSpec((1, d), lambda i: (0, 0)),
                  pl.BlockSpec((1, d), lambda i: (0, 0)),
                  pl.BlockSpec(memory_space=pl.ANY)],
        out_specs=pl.BlockSpec((PEER_TB, d), lambda i: (i, 0)),
        out_shape=jax.ShapeDtypeStruct((n, d), F32),
        scratch_shapes=([pltpu.SMEM((PEER_TB * PEER_PICKS,), I32)]
                        + [pltpu.VMEM((PEER_PICKS, d), jnp.uint32)] * GATHER_DEPTH
                        + [pltpu.SemaphoreType.DMA((GATHER_DEPTH + 1,))]),
        compiler_params=_params(("arbitrary",)),
        name="peer_gather",
    )(idx, gate, h, g2.reshape(1, d), gf.reshape(1, d), uv)


def _rel_bucket(rel):
    nb = NUM_BUCKETS // 2
    max_exact = nb // 2
    n = jnp.abs(rel)
    large = max_exact + (jnp.log(jnp.maximum(n, 1).astype(F32) / max_exact)
                         / math.log(MAX_DISTANCE / max_exact) * (nb - max_exact)).astype(I32)
    large = jnp.minimum(large, nb - 1)
    return jnp.where(rel > 0, nb, 0) + jnp.where(n < max_exact, n, large)


def _bias(rel, table):
    return jnp.transpose(table[_rel_bucket(rel)].astype(F32), (2, 0, 1))


def kernel(x_prompt, x_sample, cache_k, cache_v, state_conv, rel_bias, norm1_g, w_in, attn_sink, conv_w, conv_b,
           conv_ln_g, conv_ln_b, out_norm_attn_g, out_norm_conv_g, w_out, norm2_g, peer_wq, peer_subkeys, peer_u,
           peer_v, final_norm_g):
    batch, seq, d = x_prompt.shape
    dec_batch, dec_seq, _ = x_sample.shape
    depth = w_in.shape[0]
    n_p, n_s = batch * seq, dec_batch * dec_seq
    n = n_p + n_s
    nblk = n // PEER_TB

    h = jnp.concatenate([x_prompt.reshape(n_p, d), x_sample.reshape(n_s, d)], axis=0)

    qi = jnp.arange(CHUNK, dtype=I32)
    bias_p = _bias((jnp.arange(BAND, dtype=I32) - WINDOW)[None, :] - qi[:, None], rel_bias)
    ti = jnp.arange(dec_seq, dtype=I32)
    kpos_s = jnp.concatenate([jnp.arange(WINDOW, dtype=I32) - WINDOW, ti])
    bias_s = _bias(kpos_s[None, :] - ti[:, None], rel_bias)

    outs = {k: [] for k in ("kp", "vp", "cp", "ks", "vs", "cs")}
    for l in range(depth):
        wl = w_in[l]
        o1 = ATTN_WIDTH
        o2 = o1 + KV_WIDTH
        o3 = o2 + KV_WIDTH
        w_perm = jnp.concatenate([wl[:, :o1], wl[:, o3:], wl[:, o1:o3]], axis=1)
        z = rms_matmul(h, norm1_g[l], w_perm)

        att_p = attn_prompt(z, bias_p, attn_sink[l], batch, seq)
        att_s = attn_sample(z, cache_k[l].reshape(dec_batch, WINDOW, KV_WIDTH),
                            cache_v[l].reshape(dec_batch, WINDOW, KV_WIDTH), bias_s, attn_sink[l],
                            n_p, dec_batch, dec_seq)
        conv_p, tail_p = conv_prompt(z, conv_w[l], conv_b[l], conv_ln_g[l], conv_ln_b[l], batch, seq)
        state = jnp.pad(state_conv[l], ((0, 0), (CONV_HALO - CONV_STATE, 0), (0, 0)))
        conv_s, tail_s = conv_sample(z, state, conv_w[l], conv_b[l], conv_ln_g[l], conv_ln_b[l],
                                     n_p, dec_batch, dec_seq)
        h = merge(jnp.concatenate([att_p, att_s], axis=0), jnp.concatenate([conv_p, conv_s], axis=0),
                  out_norm_attn_g[l], out_norm_conv_g[l], w_out[l], h)

        q = rms_matmul(h, norm2_g[l], peer_wq[l])
        idx_t, gate = peer_topk(q, peer_subkeys[l])
        idx = jnp.transpose(idx_t, (0, 2, 1)).reshape(nblk, PEER_TB * PEER_PICKS)
        uv = pack_expert_tables(peer_u[l], peer_v[l])
        h = peer_gather(idx, gate, h, norm2_g[l], final_norm_g, uv, final=(l == depth - 1))

        zp = z[:n_p].reshape(batch, seq, IN_WIDTH)
        zs = z[n_p:].reshape(dec_batch, dec_seq, IN_WIDTH)
        outs["kp"].append(zp[:, -WINDOW:, K_OFF:V_OFF].reshape(batch, WINDOW, N_KV_HEADS, HEAD_DIM))
        outs["vp"].append(zp[:, -WINDOW:, V_OFF:].reshape(batch, WINDOW, N_KV_HEADS, HEAD_DIM))
        outs["cp"].append(tail_p[:, CONV_HALO - CONV_STATE:])
        outs["ks"].append(zs[:, :, K_OFF:V_OFF].reshape(dec_batch, dec_seq, N_KV_HEADS, HEAD_DIM))
        outs["vs"].append(zs[:, :, V_OFF:].reshape(dec_batch, dec_seq, N_KV_HEADS, HEAD_DIM))
        outs["cs"].append(tail_s[:, CONV_HALO - CONV_STATE:])

    y_prompt = h[:n_p].reshape(batch, seq, d)
    y_sample = h[n_p:].reshape(dec_batch, dec_seq, d)
    return (y_prompt, y_sample, jnp.stack(outs["kp"]), jnp.stack(outs["vp"]), jnp.stack(outs["cp"]),
            jnp.stack(outs["ks"]), jnp.stack(outs["vs"]), jnp.stack(outs["cs"]))
```

```python
import functools
import math

import jax
import jax.numpy as jnp
from jax import lax
from jax.experimental import pallas as pl
from jax.experimental.pallas import tpu as pltpu

F32 = jnp.float32
I32 = jnp.int32

D_MODEL = 2048
CHUNK = 64
ATTN_WIDTH = 1024
CONV_CH = 1024
HEAD_DIM = 64
N_HEADS = 16
N_KV_HEADS = 4
GROUP = N_HEADS // N_KV_HEADS
KV_WIDTH = N_KV_HEADS * HEAD_DIM
WINDOW = 128
BAND = WINDOW + CHUNK
CONV_WIDTH = 31
CONV_STATE = CONV_WIDTH - 1
CONV_HALO = 32
NUM_BUCKETS = 32
MAX_DISTANCE = 128
PEER_HEADS = 8
PEER_NKEYS = 128
PEER_HALF = 128
PEER_TOPK = 16
PEER_PICKS = PEER_HEADS * PEER_TOPK
IN_WIDTH = ATTN_WIDTH + 2 * KV_WIDTH + 2 * CONV_CH
EPS = 1e-6
ATTN_SCALE = 1.0 / math.sqrt(HEAD_DIM)
NEG_INF = -1e30

Q_OFF = 0
A_OFF = ATTN_WIDTH
G_OFF = A_OFF + CONV_CH
K_OFF = G_OFF + CONV_CH
V_OFF = K_OFF + KV_WIDTH

VMEM_LIMIT = 56 * 1024 * 1024


def _params(sem, vmem=VMEM_LIMIT):
    return pltpu.CompilerParams(dimension_semantics=sem, vmem_limit_bytes=vmem)


def _rms(x, g):
    return x * lax.rsqrt(jnp.mean(x * x, axis=-1, keepdims=True) + EPS) * g


def _rms_matmul_kernel(x_ref, g_ref, w_ref, o_ref, xn_ref):
    @pl.when(pl.program_id(1) == 0)
    def _():
        xn_ref[...] = _rms(x_ref[...], g_ref[...])

    o_ref[...] = jnp.dot(xn_ref[...], w_ref[...], preferred_element_type=F32)


def rms_matmul(x, g, w, *, tm=512, tn=512):
    n, d = x.shape
    m = w.shape[1]
    return pl.pallas_call(
        _rms_matmul_kernel,
        grid=(n // tm, m // tn),
        in_specs=[pl.BlockSpec((tm, d), lambda i, j: (i, 0)),
                  pl.BlockSpec((1, d), lambda i, j: (0, 0)),
                  pl.BlockSpec((d, tn), lambda i, j: (0, j))],
        out_specs=pl.BlockSpec((tm, tn), lambda i, j: (i, j)),
        out_shape=jax.ShapeDtypeStruct((n, m), F32),
        scratch_shapes=[pltpu.VMEM((tm, d), F32)],
        compiler_params=_params(("parallel", "arbitrary")),
        name="rms_matmul",
    )(x, g.reshape(1, d), w)


def _attn_heads(q, kk, vv, bias_ref, sink_ref, mask):
    tq = q.shape[0]
    row = lax.broadcasted_iota(I32, (GROUP * tq, 1), 0)
    scores, sinks = [], []
    for g in range(N_KV_HEADS):
        qg = jnp.concatenate([q[:, (g * GROUP + j) * HEAD_DIM:(g * GROUP + j + 1) * HEAD_DIM]
                              for j in range(GROUP)], axis=0)
        kh = kk[:, g * HEAD_DIM:(g + 1) * HEAD_DIM]
        s = lax.dot_general(qg, kh, (((1,), (1,)), ((), ())), preferred_element_type=F32)
        s = s * ATTN_SCALE + bias_ref[g]
        if mask is not None:
            s = jnp.where(mask, s, NEG_INF)
        scores.append(s)
        sk = jnp.full((GROUP * tq, 1), sink_ref[g * GROUP + GROUP - 1], F32)
        for j in range(GROUP - 2, -1, -1):
            sk = jnp.where(row < (j + 1) * tq, sink_ref[g * GROUP + j], sk)
        sinks.append(sk)
    probs = []
    for s, sk in zip(scores, sinks):
        m = jnp.maximum(jnp.max(s, axis=-1, keepdims=True), sk)
        p = jnp.exp(s - m)
        probs.append(p * (1.0 / (jnp.sum(p, axis=-1, keepdims=True) + jnp.exp(sk - m))))
    outs = []
    for g, p in enumerate(probs):
        o = jnp.dot(p, vv[:, g * HEAD_DIM:(g + 1) * HEAD_DIM], preferred_element_type=F32)
        outs += [o[j * tq:(j + 1) * tq, :] for j in range(GROUP)]
    return jnp.concatenate(outs, axis=1)


def _attn_prompt_kernel(q_ref, k0_ref, k1_ref, k2_ref, v0_ref, v1_ref, v2_ref, bias_ref, sink_ref, o_ref):
    c = pl.program_id(1)
    kk = jnp.concatenate([k0_ref[...], k1_ref[...], k2_ref[...]], axis=0)
    vv = jnp.concatenate([v0_ref[...], v1_ref[...], v2_ref[...]], axis=0)
    col = lax.broadcasted_iota(I32, (1, BAND), 1)
    mask = col + c * CHUNK - WINDOW >= 0
    o_ref[...] = _attn_heads(q_ref[...], kk, vv, bias_ref, sink_ref, mask)


def attn_prompt(z, bias, sink, batch, seq):
    n_chunks = seq // CHUNK
    kb, vb = K_OFF // KV_WIDTH, V_OFF // KV_WIDTH

    def band_spec(back, colblock):
        return pl.BlockSpec((CHUNK, KV_WIDTH),
                            lambda b, c: (b * n_chunks + jnp.maximum(c - back, 0), colblock))

    return pl.pallas_call(
        _attn_prompt_kernel,
        grid=(batch, n_chunks),
        in_specs=[pl.BlockSpec((CHUNK, ATTN_WIDTH), lambda b, c: (b * n_chunks + c, 0)),
                  band_spec(2, kb), band_spec(1, kb), band_spec(0, kb),
                  band_spec(2, vb), band_spec(1, vb), band_spec(0, vb),
                  pl.BlockSpec((N_KV_HEADS, GROUP * CHUNK, BAND), lambda b, c: (0, 0, 0)),
                  pl.BlockSpec(memory_space=pltpu.SMEM)],
        out_specs=pl.BlockSpec((CHUNK, ATTN_WIDTH), lambda b, c: (b * n_chunks + c, 0)),
        out_shape=jax.ShapeDtypeStruct((batch * seq, ATTN_WIDTH), F32),
        compiler_params=_params(("parallel", "arbitrary")),
        name="attn_prompt",
    )(z, z, z, z, z, z, z, bias, sink)


def _attn_sample_kernel(q_ref, kn_ref, vn_ref, kc_ref, vc_ref, bias_ref, sink_ref, o_ref):
    kk = jnp.concatenate([kc_ref[0], kn_ref[...]], axis=0)
    vv = jnp.concatenate([vc_ref[0], vn_ref[...]], axis=0)
    o_ref[...] = _attn_heads(q_ref[...], kk, vv, bias_ref, sink_ref, None)


def attn_sample(z, cache_k, cache_v, bias, sink, row0, dec_batch, dec_seq):
    r0 = row0 // dec_seq
    kb, vb = K_OFF // KV_WIDTH, V_OFF // KV_WIDTH
    return pl.pallas_call(
        _attn_sample_kernel,
        grid=(dec_batch,),
        in_specs=[pl.BlockSpec((dec_seq, ATTN_WIDTH), lambda b: (r0 + b, 0)),
                  pl.BlockSpec((dec_seq, KV_WIDTH), lambda b: (r0 + b, kb)),
                  pl.BlockSpec((dec_seq, KV_WIDTH), lambda b: (r0 + b, vb)),
                  pl.BlockSpec((1, WINDOW, KV_WIDTH), lambda b: (b, 0, 0)),
                  pl.BlockSpec((1, WINDOW, KV_WIDTH), lambda b: (b, 0, 0)),
                  pl.BlockSpec((N_KV_HEADS, GROUP * dec_seq, WINDOW + dec_seq), lambda b: (0, 0, 0)),
                  pl.BlockSpec(memory_space=pltpu.SMEM)],
        out_specs=pl.BlockSpec((dec_seq, ATTN_WIDTH), lambda b: (b, 0)),
        out_shape=jax.ShapeDtypeStruct((dec_batch * dec_seq, ATTN_WIDTH), F32),
        compiler_params=_params(("parallel",)),
        name="attn_sample",
    )(z, z, z, cache_k, cache_v, bias, sink)


CONV_ROWS = 32


def _conv_kernel(pa_ref, pg_ref, a_ref, g_ref, w_ref, cb_ref, lg_ref, lb_ref, o_ref, tail_ref, ext_ref,
                 *, rows, prev_is_glu):
    if prev_is_glu:
        prev = pa_ref[0]
    else:
        prev = pa_ref[...] * jax.nn.sigmoid(pg_ref[...])
        prev = jnp.where(pl.program_id(1) == 0, 0.0, prev)
    ext_ref[0:CONV_HALO, :] = prev
    ext_ref[CONV_HALO:CONV_HALO + rows, :] = a_ref[...] * jax.nn.sigmoid(g_ref[...])
    step = min(CONV_ROWS, rows)
    for r0 in range(0, rows, step):
        acc = jnp.zeros((step, CONV_CH), F32)
        for j in range(CONV_WIDTH):
            off = r0 + j + CONV_HALO - CONV_STATE
            acc = acc + w_ref[j:j + 1, :] * ext_ref[off:off + step, :]
        y = acc + cb_ref[...]
        mu = jnp.mean(y, axis=-1, keepdims=True)
        yc = y - mu
        yn = yc * lax.rsqrt(jnp.mean(yc * yc, axis=-1, keepdims=True) + EPS) * lg_ref[...] + lb_ref[...]
        o_ref[r0:r0 + step, :] = yn * jax.nn.sigmoid(yn)
    tail_ref[0] = ext_ref[rows:rows + CONV_HALO, :]


def _conv_call(kernel, grid, in_specs, out_specs, out_shape, rows, sem, name, args):
    return pl.pallas_call(
        kernel, grid=grid, in_specs=in_specs, out_specs=out_specs, out_shape=out_shape,
        scratch_shapes=[pltpu.VMEM((CONV_HALO + rows, CONV_CH), F32)],
        compiler_params=_params(sem), name=name,
    )(*args)


def _conv_param_specs(nd):
    zero = (lambda b, t: (0, 0)) if nd == 2 else (lambda b: (0, 0))
    return [pl.BlockSpec((CONV_WIDTH, CONV_CH), zero)] + [pl.BlockSpec((1, CONV_CH), zero)] * 3


def conv_prompt(z, w, cb, lg, lb, batch, seq, *, rows=256):
    tiles = seq // rows
    halo_per_tile = rows // CONV_HALO
    ab, gb = A_OFF // CONV_CH, G_OFF // CONV_CH

    def prev_spec(colblock):
        return pl.BlockSpec(
            (CONV_HALO, CONV_CH),
            lambda b, t: (jnp.maximum((b * tiles + t) * halo_per_tile - 1, 0), colblock))

    def cur_spec(colblock):
        return pl.BlockSpec((rows, CONV_CH), lambda b, t: (b * tiles + t, colblock))

    return _conv_call(
        functools.partial(_conv_kernel, rows=rows, prev_is_glu=False),
        (batch, tiles),
        [prev_spec(ab), prev_spec(gb), cur_spec(ab), cur_spec(gb)] + _conv_param_specs(2),
        [pl.BlockSpec((rows, CONV_CH), lambda b, t: (b * tiles + t, 0)),
         pl.BlockSpec((1, CONV_HALO, CONV_CH), lambda b, t: (b, 0, 0))],
        [jax.ShapeDtypeStruct((batch * seq, CONV_CH), F32),
         jax.ShapeDtypeStruct((batch, CONV_HALO, CONV_CH), F32)],
        rows, ("parallel", "arbitrary"), "conv_prompt",
        (z, z, z, z, w, cb.reshape(1, -1), lg.reshape(1, -1), lb.reshape(1, -1)))


def conv_sample(z, state, w, cb, lg, lb, row0, dec_batch, dec_seq):
    r0 = row0 // dec_seq
    ab, gb = A_OFF // CONV_CH, G_OFF // CONV_CH
    st_spec = pl.BlockSpec((1, CONV_HALO, CONV_CH), lambda b: (b, 0, 0))
    return _conv_call(
        functools.partial(_conv_kernel, rows=dec_seq, prev_is_glu=True),
        (dec_batch,),
        [st_spec, st_spec,
         pl.BlockSpec((dec_seq, CONV_CH), lambda b: (r0 + b, ab)),
         pl.BlockSpec((dec_seq, CONV_CH), lambda b: (r0 + b, gb))] + _conv_param_specs(1),
        [pl.BlockSpec((dec_seq, CONV_CH), lambda b: (b, 0)),
         pl.BlockSpec((1, CONV_HALO, CONV_CH), lambda b: (b, 0, 0))],
        [jax.ShapeDtypeStruct((dec_batch * dec_seq, CONV_CH), F32),
         jax.ShapeDtypeStruct((dec_batch, CONV_HALO, CONV_CH), F32)],
        dec_seq, ("parallel",), "conv_sample",
        (state, state, z, z, w, cb.reshape(1, -1), lg.reshape(1, -1), lb.reshape(1, -1)))


def _merge_kernel(att_ref, conv_ref, ga_ref, gc_ref, wa_ref, wc_ref, res_ref, o_ref):
    an = _rms(att_ref[...], ga_ref[...])
    cn = _rms(conv_ref[...], gc_ref[...])
    o_ref[...] = (res_ref[...] + jnp.dot(an, wa_ref[...], preferred_element_type=F32)
                  + jnp.dot(cn, wc_ref[...], preferred_element_type=F32))


def merge(att, conv, ga, gc, w_out, res, *, tm=512, tn=512):
    n = att.shape[0]
    d = w_out.shape[1]
    return pl.pallas_call(
        _merge_kernel,
        grid=(n // tm, d // tn),
        in_specs=[pl.BlockSpec((tm, ATTN_WIDTH), lambda i, j: (i, 0)),
                  pl.BlockSpec((tm, CONV_CH), lambda i, j: (i, 0)),
                  pl.BlockSpec((1, ATTN_WIDTH), lambda i, j: (0, 0)),
                  pl.BlockSpec((1, CONV_CH), lambda i, j: (0, 0)),
                  pl.BlockSpec((ATTN_WIDTH, tn), lambda i, j: (0, j)),
                  pl.BlockSpec((CONV_CH, tn), lambda i, j: (1, j)),
                  pl.BlockSpec((tm, tn), lambda i, j: (i, j))],
        out_specs=pl.BlockSpec((tm, tn), lambda i, j: (i, j)),
        out_shape=jax.ShapeDtypeStruct((n, d), F32),
        compiler_params=_params(("parallel", "arbitrary")),
        name="merge",
    )(att, conv, ga.reshape(1, -1), gc.reshape(1, -1), w_out, w_out, res)


PEER_TB = 128


def _topk_rows(chains, k, rank, payloads=None):
    chains = list(chains)
    vals = [[] for _ in chains]
    picks = [[] for _ in chains]
    for _ in range(k):
        for c, s in enumerate(chains):
            m = jnp.max(s, axis=0, keepdims=True)
            first = jnp.min(jnp.where(s == m, rank, RANK_NONE), axis=0, keepdims=True)
            hit = rank == first
            vals[c].append(m)
            picks[c].append(first if payloads is None
                            else jnp.max(jnp.where(hit, payloads[c], -1.0), axis=0, keepdims=True))
            chains[c] = jnp.where(hit, -jnp.inf, s)
    return ([jnp.concatenate(v, axis=0) for v in vals], [jnp.concatenate(p, axis=0) for p in picks])


RANK_NONE = 1e9

PAIR_SEGMENTS = ((0, 1, 0, 16), (1, 1, 0, 8), (2, 1, 0, 8), (3, 1, 0, 8),
                 (8, 8, 0, 1), (4, 4, 0, 1), (4, 4, 1, 1), (4, 1, 2, 1))
PAIR_ROWS = 64


def _pair_rows(a, b, combine):
    return [combine(a[i0:i0 + ni, :], b[j0:j0 + nj, :]) for i0, ni, j0, nj in PAIR_SEGMENTS]


def pair_ranks():
    rows = []
    for i0, ni, j0, nj in PAIR_SEGMENTS:
        rows += [float(i * PEER_TOPK + j) for i in range(i0, i0 + ni) for j in range(j0, j0 + nj)]
    rows += [RANK_NONE] * (PAIR_ROWS - len(rows))
    return jnp.broadcast_to(jnp.asarray(rows, F32)[:, None], (PAIR_ROWS, PEER_TB))


TOPK_HEADS = 2


def _peer_topk_kernel(q_ref, keys_ref, prank_ref, idx_ref, gate_ref):
    q = q_ref[...]
    dn = (((1,), (1,)), ((), ()))
    scores = [lax.dot_general(keys_ref[c % 2], q[:, c * PEER_HALF:(c + 1) * PEER_HALF], dn,
                              preferred_element_type=F32) for c in range(2 * TOPK_HEADS)]
    key_rank = lax.broadcasted_iota(I32, scores[0].shape, 0).astype(F32)
    vals, keys = _topk_rows(scores, PEER_TOPK, key_rank)
    pad = PAIR_ROWS - sum(ni * nj for _, ni, _, nj in PAIR_SEGMENTS)
    fill = jnp.full((pad, PEER_TB), -jnp.inf, F32)
    cand = [jnp.concatenate(_pair_rows(vals[2 * h], vals[2 * h + 1], lambda x, y: x + y) + [fill], axis=0)
            for h in range(TOPK_HEADS)]
    expert = [jnp.concatenate(_pair_rows(keys[2 * h], keys[2 * h + 1], lambda x, y: x * PEER_NKEYS + y) + [fill],
                              axis=0) for h in range(TOPK_HEADS)]
    cvs, picks = _topk_rows(cand, PEER_TOPK, prank_ref[...], payloads=expert)
    for h in range(TOPK_HEADS):
        e = jnp.exp(cvs[h] - cvs[h][0:1, :])
        idx_ref[0, h * PEER_TOPK:(h + 1) * PEER_TOPK, :] = picks[h].astype(I32)
        gate_ref[0, h * PEER_TOPK:(h + 1) * PEER_TOPK, :] = e / jnp.sum(e, axis=0, keepdims=True)


def peer_topk(q, subkeys):
    n = q.shape[0]
    nblk = n // PEER_TB
    out = jax.ShapeDtypeStruct((nblk, PEER_PICKS, PEER_TB), I32)
    return pl.pallas_call(
        _peer_topk_kernel,
        grid=(nblk, PEER_HEADS // TOPK_HEADS),
        in_specs=[pl.BlockSpec((PEER_TB, TOPK_HEADS * 2 * PEER_HALF), lambda i, h: (i, h)),
                  pl.BlockSpec((2, PEER_NKEYS, PEER_HALF), lambda i, h: (0, 0, 0)),
                  pl.BlockSpec((PAIR_ROWS, PEER_TB), lambda i, h: (0, 0))],
        out_specs=[pl.BlockSpec((1, TOPK_HEADS * PEER_TOPK, PEER_TB), lambda i, h: (i, h, 0))] * 2,
        out_shape=[out, jax.ShapeDtypeStruct(out.shape, F32)],
        compiler_params=_params(("parallel", "arbitrary")),
        name="peer_topk",
    )(q, subkeys, pair_ranks())


def _gelu_tanh(x):
    return 0.5 * x * (1.0 + jnp.tanh(math.sqrt(2.0 / math.pi) * (x + 0.044715 * x * x * x)))


GATHER_DEPTH = 8
TOKENS_PER_STEP = 2
LANES = 128


def _peer_gather_kernel(idx_hbm, gate_ref, h_ref, g2_ref, gf_ref, uv_hbm, o_ref, idx_ref, *scratch, final):
    bufs, sems = scratch[:GATHER_DEPTH], scratch[GATHER_DEPTH]
    blk = pl.program_id(0)
    idx_copy = pltpu.make_async_copy(idx_hbm.at[blk], idx_ref, sems.at[GATHER_DEPTH])
    idx_copy.start()
    idx_copy.wait()

    def start_rows(t, slot):
        base = t * PEER_PICKS
        for k in range(PEER_PICKS):
            e = idx_ref[base + k]
            pltpu.make_async_copy(uv_hbm.at[e], bufs[slot].at[pl.ds(k, 1)], sems.at[slot]).start(priority=k % 2)

    def wait_rows(slot):
        pltpu.make_async_copy(uv_hbm.at[pl.ds(0, PEER_PICKS), 0], bufs[slot], sems.at[slot]).wait()

    lane = lax.broadcasted_iota(I32, (PEER_PICKS, PEER_TB), 1)
    half = D_MODEL // 2

    def unpack(words):
        lo = lax.bitcast_convert_type(words << 16, F32)
        hi = lax.bitcast_convert_type(words & jnp.uint32(0xFFFF0000), F32)
        return lo, hi

    def compute(t, slot):
        buf = bufs[slot]
        hrow = h_ref[pl.ds(t, 1), :]
        xn = _rms(hrow, g2_ref[...])
        acc = jnp.zeros((PEER_PICKS, LANES), F32)
        for c in range(half // LANES):
            u_lo, u_hi = unpack(buf[:, c * LANES:(c + 1) * LANES])
            acc = acc + (u_lo * xn[:, c * LANES:(c + 1) * LANES]
                         + u_hi * xn[:, half + c * LANES:half + (c + 1) * LANES])
        act = _gelu_tanh(jnp.sum(acc, axis=1, keepdims=True))
        gate = jnp.sum(jnp.where(lane == t, gate_ref[0], 0.0), axis=1, keepdims=True)
        w = jnp.broadcast_to(gate * act, (PEER_PICKS, LANES))
        mix_lo, mix_hi = [], []
        for c in range(half // LANES):
            v_lo, v_hi = unpack(buf[:, half + c * LANES:half + (c + 1) * LANES])
            mix_lo.append(jnp.sum(w * v_lo, axis=0, keepdims=True))
            mix_hi.append(jnp.sum(w * v_hi, axis=0, keepdims=True))
        y = hrow + jnp.concatenate(mix_lo + mix_hi, axis=1)
        if final:
            y = _rms(y, gf_ref[...])
        o_ref[pl.ds(t, 1), :] = y

    ahead = GATHER_DEPTH - TOKENS_PER_STEP

    def group(j, last):
        for b in range(0, GATHER_DEPTH, TOKENS_PER_STEP):
            t = j * GATHER_DEPTH + b
            for i in range(TOKENS_PER_STEP):
                wait_rows(b + i)
            for i in range(TOKENS_PER_STEP):
                if not last or b + i + ahead < GATHER_DEPTH:
                    start_rows(t + i + ahead, (b + i + ahead) % GATHER_DEPTH)
            for i in range(TOKENS_PER_STEP):
                compute(t + i, b + i)

    for b in range(ahead):
        start_rows(b, b)
    groups = PEER_TB // GATHER_DEPTH

    def body(j, carry):
        group(j, False)
        return carry

    lax.fori_loop(0, groups - 1, body, 0)
    group(groups - 1, True)


def pack_expert_tables(u, v):
    def pack(x):
        bits = lax.bitcast_convert_type(x.astype(jnp.bfloat16), jnp.uint16).astype(jnp.uint32)
        half = x.shape[1] // 2
        return bits[:, :half] | (bits[:, half:] << 16)

    return jnp.concatenate([pack(u), pack(v)], axis=1)[:, None, :]


def peer_gather(idx, gate, h, g2, gf, uv, *, final):
    n, d = h.shape
    nblk = n // PEER_TB
    return pl.pallas_call(
        functools.partial(_peer_gather_kernel, final=final),
        grid=(nblk,),
        in_specs=[pl.BlockSpec(memory_space=pl.ANY),
                  pl.BlockSpec((1, PEER_PICKS, PEER_TB), lambda i: (i, 0, 0)),
                  pl.BlockSpec((PEER_TB, d), lambda i: (i, 0)),
                  pl.BlockSpec((1, d), lambda i: (0, 0)),
                  pl.BlockSpec((1, d), lambda i: (0, 0)),
                  pl.BlockSpec(memory_space=pl.ANY)],
        out_specs=pl.BlockSpec((PEER_TB, d), lambda i: (i, 0)),
        out_shape=jax.ShapeDtypeStruct((n, d), F32),
        scratch_shapes=([pltpu.SMEM((PEER_TB * PEER_PICKS,), I32)]
                        + [pltpu.VMEM((PEER_PICKS, d), jnp.uint32)] * GATHER_DEPTH
                        + [pltpu.SemaphoreType.DMA((GATHER_DEPTH + 1,))]),
        compiler_params=_params(("arbitrary",)),
        name="peer_gather",
    )(idx, gate, h, g2.reshape(1, d), gf.reshape(1, d), uv)


def _rel_bucket(rel):
    nb = NUM_BUCKETS // 2
    max_exact = nb // 2
    n = jnp.abs(rel)
    large = max_exact + (jnp.log(jnp.maximum(n, 1).astype(F32) / max_exact)
                         / math.log(MAX_DISTANCE / max_exact) * (nb - max_exact)).astype(I32)
    large = jnp.minimum(large, nb - 1)
    return jnp.where(rel > 0, nb, 0) + jnp.where(n < max_exact, n, large)


def _bias(rel, table):
    nq, nk = rel.shape
    return jnp.transpose(table[_rel_bucket(rel)].astype(F32), (2, 0, 1)).reshape(N_KV_HEADS, GROUP * nq, nk)


def kernel(x_prompt, x_sample, cache_k, cache_v, state_conv, rel_bias, norm1_g, w_in, attn_sink, conv_w, conv_b,
           conv_ln_g, conv_ln_b, out_norm_attn_g, out_norm_conv_g, w_out, norm2_g, peer_wq, peer_subkeys, peer_u,
           peer_v, final_norm_g):
    batch, seq, d = x_prompt.shape
    dec_batch, dec_seq, _ = x_sample.shape
    depth = w_in.shape[0]
    n_p, n_s = batch * seq, dec_batch * dec_seq
    n = n_p + n_s
    nblk = n // PEER_TB

    h = jnp.concatenate([x_prompt.reshape(n_p, d), x_sample.reshape(n_s, d)], axis=0)

    qi = jnp.arange(CHUNK, dtype=I32)
    bias_p = _bias((jnp.arange(BAND, dtype=I32) - WINDOW)[None, :] - qi[:, None], rel_bias)
    ti = jnp.arange(dec_seq, dtype=I32)
    kpos_s = jnp.concatenate([jnp.arange(WINDOW, dtype=I32) - WINDOW, ti])
    bias_s = _bias(kpos_s[None, :] - ti[:, None], rel_bias)

    outs = {k: [] for k in ("kp", "vp", "cp", "ks", "vs", "cs")}
    for l in range(depth):
        wl = w_in[l]
        o1 = ATTN_WIDTH
        o2 = o1 + KV_WIDTH
        o3 = o2 + KV_WIDTH
        w_perm = jnp.concatenate([wl[:, :o1], wl[:, o3:], wl[:, o1:o3]], axis=1)
        z = rms_matmul(h, norm1_g[l], w_perm)

        att_p = attn_prompt(z, bias_p, attn_sink[l], batch, seq)
        att_s = attn_sample(z, cache_k[l].reshape(dec_batch, WINDOW, KV_WIDTH),
                            cache_v[l].reshape(dec_batch, WINDOW, KV_WIDTH), bias_s, attn_sink[l],
                            n_p, dec_batch, dec_seq)
        conv_p, tail_p = conv_prompt(z, conv_w[l], conv_b[l], conv_ln_g[l], conv_ln_b[l], batch, seq)
        state = jnp.pad(state_conv[l], ((0, 0), (CONV_HALO - CONV_STATE, 0), (0, 0)))
        conv_s, tail_s = conv_sample(z, state, conv_w[l], conv_b[l], conv_ln_g[l], conv_ln_b[l],
                                     n_p, dec_batch, dec_seq)
        h = merge(jnp.concatenate([att_p, att_s], axis=0), jnp.concatenate([conv_p, conv_s], axis=0),
                  out_norm_attn_g[l], out_norm_conv_g[l], w_out[l], h)

        q = rms_matmul(h, norm2_g[l], peer_wq[l])
        idx_t, gate = peer_topk(q, peer_subkeys[l])
        idx = jnp.transpose(idx_t, (0, 2, 1)).reshape(nblk, PEER_TB * PEER_PICKS)
        uv = pack_expert_tables(peer_u[l], peer_v[l])
        h = peer_gather(idx, gate, h, norm2_g[l], final_norm_g, uv, final=(l == depth - 1))

        zp = z[:n_p].reshape(batch, seq, IN_WIDTH)
        zs = z[n_p:].reshape(dec_batch, dec_seq, IN_WIDTH)
        outs["kp"].append(zp[:, -WINDOW:, K_OFF:V_OFF].reshape(batch, WINDOW, N_KV_HEADS, HEAD_DIM))
        outs["vp"].append(zp[:, -WINDOW:, V_OFF:].reshape(batch, WINDOW, N_KV_HEADS, HEAD_DIM))
        outs["cp"].append(tail_p[:, CONV_HALO - CONV_STATE:])
        outs["ks"].append(zs[:, :, K_OFF:V_OFF].reshape(dec_batch, dec_seq, N_KV_HEADS, HEAD_DIM))
        outs["vs"].append(zs[:, :, V_OFF:].reshape(dec_batch, dec_seq, N_KV_HEADS, HEAD_DIM))
        outs["cs"].append(tail_s[:, CONV_HALO - CONV_STATE:])

    y_prompt = h[:n_p].reshape(batch, seq, d)
    y_sample = h[n_p:].reshape(dec_batch, dec_seq, d)
    return (y_prompt, y_sample, jnp.stack(outs["kp"]), jnp.stack(outs["vp"]), jnp.stack(outs["cp"]),
            jnp.stack(outs["ks"]), jnp.stack(outs["vs"]), jnp.stack(outs["cs"]))
```

```python
import functools
import math

import jax
import jax.numpy as jnp
from jax import lax
from jax.experimental import pallas as pl
from jax.experimental.pallas import tpu as pltpu

F32 = jnp.float32
BF16 = jnp.bfloat16
I32 = jnp.int32

D_MODEL = 2048
CHUNK = 64
ATTN_WIDTH = 1024
CONV_CH = 1024
HEAD_DIM = 64
N_HEADS = 16
N_KV_HEADS = 4
GROUP = N_HEADS // N_KV_HEADS
KV_WIDTH = N_KV_HEADS * HEAD_DIM
WINDOW = 128
BAND = WINDOW + CHUNK
CONV_WIDTH = 31
CONV_STATE = CONV_WIDTH - 1
CONV_HALO = 32
NUM_BUCKETS = 32
MAX_DISTANCE = 128
PEER_HEADS = 8
PEER_NKEYS = 128
PEER_HALF = 128
PEER_TOPK = 16
PEER_PICKS = PEER_HEADS * PEER_TOPK
IN_WIDTH = ATTN_WIDTH + 2 * KV_WIDTH + 2 * CONV_CH
EPS = 1e-6
ATTN_SCALE = 1.0 / math.sqrt(HEAD_DIM)
NEG_INF = -1e30

Q_OFF = 0
A_OFF = ATTN_WIDTH
G_OFF = A_OFF + CONV_CH
K_OFF = G_OFF + CONV_CH
V_OFF = K_OFF + KV_WIDTH

VMEM_LIMIT = 56 * 1024 * 1024


def _params(sem, vmem=VMEM_LIMIT):
    return pltpu.CompilerParams(dimension_semantics=sem, vmem_limit_bytes=vmem)


def _rms(x, g):
    return x * lax.rsqrt(jnp.mean(x * x, axis=-1, keepdims=True) + EPS) * g


def _rms_matmul_kernel(x_ref, g_ref, w_ref, o_ref, xn_ref):
    @pl.when(pl.program_id(1) == 0)
    def _():
        xn_ref[...] = _rms(x_ref[...], g_ref[...]).astype(BF16)

    o_ref[...] = jnp.dot(xn_ref[...], w_ref[...], preferred_element_type=F32)


def rms_matmul(x, g, w, *, tm=512, tn):
    n, d = x.shape
    m = w.shape[1]
    return pl.pallas_call(
        _rms_matmul_kernel,
        grid=(n // tm, m // tn),
        in_specs=[pl.BlockSpec((tm, d), lambda i, j: (i, 0)),
                  pl.BlockSpec((1, d), lambda i, j: (0, 0)),
                  pl.BlockSpec((d, tn), lambda i, j: (0, j))],
        out_specs=pl.BlockSpec((tm, tn), lambda i, j: (i, j)),
        out_shape=jax.ShapeDtypeStruct((n, m), F32),
        scratch_shapes=[pltpu.VMEM((tm, d), BF16)],
        compiler_params=_params(("parallel", "arbitrary")),
        name="rms_matmul",
    )(x, g.reshape(1, d), w)


def _attn_heads(q, kk, vv, bias_ref, sink_ref, mask):
    tq = q.shape[0]
    row = lax.broadcasted_iota(I32, (GROUP * tq, 1), 0)
    scores, sinks = [], []
    for g in range(N_KV_HEADS):
        qg = jnp.concatenate([q[:, (g * GROUP + j) * HEAD_DIM:(g * GROUP + j + 1) * HEAD_DIM]
                              for j in range(GROUP)], axis=0)
        kh = kk[:, g * HEAD_DIM:(g + 1) * HEAD_DIM]
        s = lax.dot_general(qg, kh, (((1,), (1,)), ((), ())), preferred_element_type=F32)
        s = s * ATTN_SCALE + bias_ref[g]
        if mask is not None:
            s = jnp.where(mask, s, NEG_INF)
        scores.append(s)
        sk = jnp.full((GROUP * tq, 1), sink_ref[g * GROUP + GROUP - 1], F32)
        for j in range(GROUP - 2, -1, -1):
            sk = jnp.where(row < (j + 1) * tq, sink_ref[g * GROUP + j], sk)
        sinks.append(sk)
    probs = []
    for s, sk in zip(scores, sinks):
        m = jnp.maximum(jnp.max(s, axis=-1, keepdims=True), sk)
        p = jnp.exp(s - m)
        probs.append(p * (1.0 / (jnp.sum(p, axis=-1, keepdims=True) + jnp.exp(sk - m))))
    outs = []
    for g, p in enumerate(probs):
        o = jnp.dot(p, vv[:, g * HEAD_DIM:(g + 1) * HEAD_DIM], preferred_element_type=F32)
        outs += [o[j * tq:(j + 1) * tq, :] for j in range(GROUP)]
    return jnp.concatenate(outs, axis=1)


def _attn_prompt_kernel(q_ref, k0_ref, k1_ref, k2_ref, v0_ref, v1_ref, v2_ref, bias_ref, sink_ref, o_ref):
    c = pl.program_id(1)
    kk = jnp.concatenate([k0_ref[...], k1_ref[...], k2_ref[...]], axis=0)
    vv = jnp.concatenate([v0_ref[...], v1_ref[...], v2_ref[...]], axis=0)
    col = lax.broadcasted_iota(I32, (1, BAND), 1)
    mask = col + c * CHUNK - WINDOW >= 0
    o_ref[...] = _attn_heads(q_ref[...], kk, vv, bias_ref, sink_ref, mask)


def attn_prompt(z, bias, sink, batch, seq):
    n_chunks = seq // CHUNK
    kb, vb = K_OFF // KV_WIDTH, V_OFF // KV_WIDTH

    def band_spec(back, colblock):
        return pl.BlockSpec((CHUNK, KV_WIDTH),
                            lambda b, c: (b * n_chunks + jnp.maximum(c - back, 0), colblock))

    return pl.pallas_call(
        _attn_prompt_kernel,
        grid=(batch, n_chunks),
        in_specs=[pl.BlockSpec((CHUNK, ATTN_WIDTH), lambda b, c: (b * n_chunks + c, 0)),
                  band_spec(2, kb), band_spec(1, kb), band_spec(0, kb),
                  band_spec(2, vb), band_spec(1, vb), band_spec(0, vb),
                  pl.BlockSpec((N_KV_HEADS, GROUP * CHUNK, BAND), lambda b, c: (0, 0, 0)),
                  pl.BlockSpec(memory_space=pltpu.SMEM)],
        out_specs=pl.BlockSpec((CHUNK, ATTN_WIDTH), lambda b, c: (b * n_chunks + c, 0)),
        out_shape=jax.ShapeDtypeStruct((batch * seq, ATTN_WIDTH), F32),
        compiler_params=_params(("parallel", "arbitrary")),
        name="attn_prompt",
    )(z, z, z, z, z, z, z, bias, sink)


def _attn_sample_kernel(q_ref, kn_ref, vn_ref, kc_ref, vc_ref, bias_ref, sink_ref, o_ref):
    kk = jnp.concatenate([kc_ref[0], kn_ref[...]], axis=0)
    vv = jnp.concatenate([vc_ref[0], vn_ref[...]], axis=0)
    o_ref[...] = _attn_heads(q_ref[...], kk, vv, bias_ref, sink_ref, None)


def attn_sample(z, cache_k, cache_v, bias, sink, row0, dec_batch, dec_seq):
    r0 = row0 // dec_seq
    kb, vb = K_OFF // KV_WIDTH, V_OFF // KV_WIDTH
    return pl.pallas_call(
        _attn_sample_kernel,
        grid=(dec_batch,),
        in_specs=[pl.BlockSpec((dec_seq, ATTN_WIDTH), lambda b: (r0 + b, 0)),
                  pl.BlockSpec((dec_seq, KV_WIDTH), lambda b: (r0 + b, kb)),
                  pl.BlockSpec((dec_seq, KV_WIDTH), lambda b: (r0 + b, vb)),
                  pl.BlockSpec((1, WINDOW, KV_WIDTH), lambda b: (b, 0, 0)),
                  pl.BlockSpec((1, WINDOW, KV_WIDTH), lambda b: (b, 0, 0)),
                  pl.BlockSpec((N_KV_HEADS, GROUP * dec_seq, WINDOW + dec_seq), lambda b: (0, 0, 0)),
                  pl.BlockSpec(memory_space=pltpu.SMEM)],
        out_specs=pl.BlockSpec((dec_seq, ATTN_WIDTH), lambda b: (b, 0)),
        out_shape=jax.ShapeDtypeStruct((dec_batch * dec_seq, ATTN_WIDTH), F32),
        compiler_params=_params(("parallel",)),
        name="attn_sample",
    )(z, z, z, cache_k, cache_v, bias, sink)


CONV_ROWS = 32


def _conv_kernel(pa_ref, pg_ref, a_ref, g_ref, w_ref, cb_ref, lg_ref, lb_ref, o_ref, tail_ref, ext_ref,
                 *, rows, prev_is_glu):
    if prev_is_glu:
        prev = pa_ref[0]
    else:
        prev = pa_ref[...] * jax.nn.sigmoid(pg_ref[...])
        prev = jnp.where(pl.program_id(1) == 0, 0.0, prev)
    ext_ref[0:CONV_HALO, :] = prev
    ext_ref[CONV_HALO:CONV_HALO + rows, :] = a_ref[...] * jax.nn.sigmoid(g_ref[...])
    step = min(CONV_ROWS, rows)
    for r0 in range(0, rows, step):
        acc = jnp.zeros((step, CONV_CH), F32)
        for j in range(CONV_WIDTH):
            off = r0 + j + CONV_HALO - CONV_STATE
            acc = acc + w_ref[j:j + 1, :] * ext_ref[off:off + step, :]
        y = acc + cb_ref[...]
        mu = jnp.mean(y, axis=-1, keepdims=True)
        yc = y - mu
        yn = yc * lax.rsqrt(jnp.mean(yc * yc, axis=-1, keepdims=True) + EPS) * lg_ref[...] + lb_ref[...]
        o_ref[r0:r0 + step, :] = yn * jax.nn.sigmoid(yn)
    tail_ref[0] = ext_ref[rows:rows + CONV_HALO, :]


def _conv_call(kernel, grid, in_specs, out_specs, out_shape, rows, sem, name, args):
    return pl.pallas_call(
        kernel, grid=grid, in_specs=in_specs, out_specs=out_specs, out_shape=out_shape,
        scratch_shapes=[pltpu.VMEM((CONV_HALO + rows, CONV_CH), F32)],
        compiler_params=_params(sem), name=name,
    )(*args)


def _conv_param_specs(nd):
    zero = (lambda b, t: (0, 0)) if nd == 2 else (lambda b: (0, 0))
    return [pl.BlockSpec((CONV_WIDTH, CONV_CH), zero)] + [pl.BlockSpec((1, CONV_CH), zero)] * 3


def conv_prompt(z, w, cb, lg, lb, batch, seq, *, rows=256):
    tiles = seq // rows
    halo_per_tile = rows // CONV_HALO
    ab, gb = A_OFF // CONV_CH, G_OFF // CONV_CH

    def prev_spec(colblock):
        return pl.BlockSpec(
            (CONV_HALO, CONV_CH),
            lambda b, t: (jnp.maximum((b * tiles + t) * halo_per_tile - 1, 0), colblock))

    def cur_spec(colblock):
        return pl.BlockSpec((rows, CONV_CH), lambda b, t: (b * tiles + t, colblock))

    return _conv_call(
        functools.partial(_conv_kernel, rows=rows, prev_is_glu=False),
        (batch, tiles),
        [prev_spec(ab), prev_spec(gb), cur_spec(ab), cur_spec(gb)] + _conv_param_specs(2),
        [pl.BlockSpec((rows, CONV_CH), lambda b, t: (b * tiles + t, 0)),
         pl.BlockSpec((1, CONV_HALO, CONV_CH), lambda b, t: (b, 0, 0))],
        [jax.ShapeDtypeStruct((batch * seq, CONV_CH), F32),
         jax.ShapeDtypeStruct((batch, CONV_HALO, CONV_CH), F32)],
        rows, ("parallel", "arbitrary"), "conv_prompt",
        (z, z, z, z, w, cb.reshape(1, -1), lg.reshape(1, -1), lb.reshape(1, -1)))


def conv_sample(z, state, w, cb, lg, lb, row0, dec_batch, dec_seq):
    r0 = row0 // dec_seq
    ab, gb = A_OFF // CONV_CH, G_OFF // CONV_CH
    st_spec = pl.BlockSpec((1, CONV_HALO, CONV_CH), lambda b: (b, 0, 0))
    return _conv_call(
        functools.partial(_conv_kernel, rows=dec_seq, prev_is_glu=True),
        (dec_batch,),
        [st_spec, st_spec,
         pl.BlockSpec((dec_seq, CONV_CH), lambda b: (r0 + b, ab)),
         pl.BlockSpec((dec_seq, CONV_CH), lambda b: (r0 + b, gb))] + _conv_param_specs(1),
        [pl.BlockSpec((dec_seq, CONV_CH), lambda b: (b, 0)),
         pl.BlockSpec((1, CONV_HALO, CONV_CH), lambda b: (b, 0, 0))],
        [jax.ShapeDtypeStruct((dec_batch * dec_seq, CONV_CH), F32),
         jax.ShapeDtypeStruct((dec_batch, CONV_HALO, CONV_CH), F32)],
        dec_seq, ("parallel",), "conv_sample",
        (state, state, z, z, w, cb.reshape(1, -1), lg.reshape(1, -1), lb.reshape(1, -1)))


def _merge_kernel(ap_ref, as_ref, cp_ref, cs_ref, ga_ref, gc_ref, wa_ref, wc_ref, res_ref, o_ref, an_ref, cn_ref,
                  *, prompt_tiles):
    first = pl.program_id(1) == 0
    is_prompt = pl.program_id(0) < prompt_tiles

    def normalise(att_ref, conv_ref):
        an_ref[...] = _rms(att_ref[...], ga_ref[...]).astype(BF16)
        cn_ref[...] = _rms(conv_ref[...], gc_ref[...]).astype(BF16)

    pl.when(first & is_prompt)(lambda: normalise(ap_ref, cp_ref))
    pl.when(first & jnp.logical_not(is_prompt))(lambda: normalise(as_ref, cs_ref))
    o_ref[...] = (res_ref[...] + jnp.dot(an_ref[...], wa_ref[...], preferred_element_type=F32)
                  + jnp.dot(cn_ref[...], wc_ref[...], preferred_element_type=F32))


def merge(att_p, att_s, conv_p, conv_s, ga, gc, w_out, res, *, tm=512, tn=1024):
    n = res.shape[0]
    d = w_out.shape[1]
    prompt_tiles = att_p.shape[0] // tm

    def rows_p(i, j):
        return (jnp.minimum(i, prompt_tiles - 1), 0)

    def rows_s(i, j):
        return (jnp.maximum(i - prompt_tiles, 0), 0)

    return pl.pallas_call(
        functools.partial(_merge_kernel, prompt_tiles=prompt_tiles),
        grid=(n // tm, d // tn),
        in_specs=[pl.BlockSpec((tm, ATTN_WIDTH), rows_p),
                  pl.BlockSpec((tm, ATTN_WIDTH), rows_s),
                  pl.BlockSpec((tm, CONV_CH), rows_p),
                  pl.BlockSpec((tm, CONV_CH), rows_s),
                  pl.BlockSpec((1, ATTN_WIDTH), lambda i, j: (0, 0)),
                  pl.BlockSpec((1, CONV_CH), lambda i, j: (0, 0)),
                  pl.BlockSpec((ATTN_WIDTH, tn), lambda i, j: (0, j)),
                  pl.BlockSpec((CONV_CH, tn), lambda i, j: (1, j)),
                  pl.BlockSpec((tm, tn), lambda i, j: (i, j))],
        out_specs=pl.BlockSpec((tm, tn), lambda i, j: (i, j)),
        out_shape=jax.ShapeDtypeStruct((n, d), F32),
        scratch_shapes=[pltpu.VMEM((tm, ATTN_WIDTH), BF16), pltpu.VMEM((tm, CONV_CH), BF16)],
        compiler_params=_params(("parallel", "arbitrary")),
        name="merge",
    )(att_p, att_s, conv_p, conv_s, ga.reshape(1, -1), gc.reshape(1, -1), w_out, w_out, res)


def _rms_kernel(x_ref, g_ref, o_ref):
    o_ref[...] = _rms(x_ref[...], g_ref[...])


def rms_rows(x, g, *, tm=512):
    n, d = x.shape
    return pl.pallas_call(
        _rms_kernel,
        grid=(n // tm,),
        in_specs=[pl.BlockSpec((tm, d), lambda i: (i, 0)), pl.BlockSpec((1, d), lambda i: (0, 0))],
        out_specs=pl.BlockSpec((tm, d), lambda i: (i, 0)),
        out_shape=jax.ShapeDtypeStruct((n, d), F32),
        compiler_params=_params(("parallel",)),
        name="rms_rows",
    )(x, g.reshape(1, d))


PEER_TB = 128


def _topk_rows(chains, k, rank, payloads=None):
    chains = list(chains)
    vals = [[] for _ in chains]
    picks = [[] for _ in chains]
    for _ in range(k):
        for c, s in enumerate(chains):
            m = jnp.max(s, axis=0, keepdims=True)
            first = jnp.min(jnp.where(s == m, rank, RANK_NONE), axis=0, keepdims=True)
            hit = rank == first
            vals[c].append(m)
            picks[c].append(first if payloads is None
                            else jnp.max(jnp.where(hit, payloads[c], -1.0), axis=0, keepdims=True))
            chains[c] = jnp.where(hit, -jnp.inf, s)
    return ([jnp.concatenate(v, axis=0) for v in vals], [jnp.concatenate(p, axis=0) for p in picks])


RANK_NONE = 1e9

PAIR_SEGMENTS = ((0, 1, 0, 16), (1, 1, 0, 8), (2, 1, 0, 8), (3, 1, 0, 8),
                 (8, 8, 0, 1), (4, 4, 0, 1), (4, 4, 1, 1), (4, 1, 2, 1))
PAIR_ROWS = 64


def _pair_rows(a, b, combine):
    return [combine(a[i0:i0 + ni, :], b[j0:j0 + nj, :]) for i0, ni, j0, nj in PAIR_SEGMENTS]


def pair_ranks():
    rows = []
    for i0, ni, j0, nj in PAIR_SEGMENTS:
        rows += [float(i * PEER_TOPK + j) for i in range(i0, i0 + ni) for j in range(j0, j0 + nj)]
    rows += [RANK_NONE] * (PAIR_ROWS - len(rows))
    return jnp.broadcast_to(jnp.asarray(rows, F32)[:, None], (PAIR_ROWS, PEER_TB))


TOPK_HEADS = 2


def _peer_topk_kernel(q_ref, keys_ref, prank_ref, idx_ref, gate_ref):
    q = q_ref[...]
    dn = (((1,), (1,)), ((), ()))
    scores = [lax.dot_general(keys_ref[c % 2], q[:, c * PEER_HALF:(c + 1) * PEER_HALF], dn,
                              preferred_element_type=F32) for c in range(2 * TOPK_HEADS)]
    key_rank = lax.broadcasted_iota(I32, scores[0].shape, 0).astype(F32)
    vals, keys = _topk_rows(scores, PEER_TOPK, key_rank)
    pad = PAIR_ROWS - sum(ni * nj for _, ni, _, nj in PAIR_SEGMENTS)
    fill = jnp.full((pad, PEER_TB), -jnp.inf, F32)
    cand = [jnp.concatenate(_pair_rows(vals[2 * h], vals[2 * h + 1], lambda x, y: x + y) + [fill], axis=0)
            for h in range(TOPK_HEADS)]
    expert = [jnp.concatenate(_pair_rows(keys[2 * h], keys[2 * h + 1], lambda x, y: x * PEER_NKEYS + y) + [fill],
                              axis=0) for h in range(TOPK_HEADS)]
    cvs, picks = _topk_rows(cand, PEER_TOPK, prank_ref[...], payloads=expert)
    for h in range(TOPK_HEADS):
        e = jnp.exp(cvs[h] - cvs[h][0:1, :])
        idx_ref[0, h * PEER_TOPK:(h + 1) * PEER_TOPK, :] = picks[h].astype(I32)
        gate_ref[0, h * PEER_TOPK:(h + 1) * PEER_TOPK, :] = e / jnp.sum(e, axis=0, keepdims=True)


def peer_topk(q, subkeys):
    n = q.shape[0]
    nblk = n // PEER_TB
    out = jax.ShapeDtypeStruct((nblk, PEER_PICKS, PEER_TB), I32)
    return pl.pallas_call(
        _peer_topk_kernel,
        grid=(nblk, PEER_HEADS // TOPK_HEADS),
        in_specs=[pl.BlockSpec((PEER_TB, TOPK_HEADS * 2 * PEER_HALF), lambda i, h: (i, h)),
                  pl.BlockSpec((2, PEER_NKEYS, PEER_HALF), lambda i, h: (0, 0, 0)),
                  pl.BlockSpec((PAIR_ROWS, PEER_TB), lambda i, h: (0, 0))],
        out_specs=[pl.BlockSpec((1, TOPK_HEADS * PEER_TOPK, PEER_TB), lambda i, h: (i, h, 0))] * 2,
        out_shape=[out, jax.ShapeDtypeStruct(out.shape, F32)],
        compiler_params=_params(("parallel", "arbitrary")),
        name="peer_topk",
    )(q, subkeys, pair_ranks())


def _gelu_tanh(x):
    return 0.5 * x * (1.0 + jnp.tanh(math.sqrt(2.0 / math.pi) * (x + 0.044715 * x * x * x)))


GATHER_DEPTH = 8
TOKENS_PER_STEP = 2
LANES = 128


def _peer_gather_kernel(idx_hbm, gate_ref, h_ref, g2_ref, uv_hbm, o_ref, idx_ref, *scratch):
    bufs, sems = scratch[:GATHER_DEPTH], scratch[GATHER_DEPTH]
    blk = pl.program_id(0)
    idx_copy = pltpu.make_async_copy(idx_hbm.at[blk], idx_ref, sems.at[GATHER_DEPTH])
    idx_copy.start()
    idx_copy.wait()

    def start_rows(t, slot):
        base = t * PEER_PICKS
        for k in range(PEER_PICKS):
            e = idx_ref[base + k]
            pltpu.make_async_copy(uv_hbm.at[e], bufs[slot].at[pl.ds(k, 1)], sems.at[slot]).start(priority=k % 2)

    def wait_rows(slot):
        pltpu.make_async_copy(uv_hbm.at[pl.ds(0, PEER_PICKS), 0], bufs[slot], sems.at[slot]).wait()

    lane = lax.broadcasted_iota(I32, (PEER_PICKS, PEER_TB), 1)
    half = D_MODEL // 2

    def unpack(words):
        lo = lax.bitcast_convert_type(words << 16, F32)
        hi = lax.bitcast_convert_type(words & jnp.uint32(0xFFFF0000), F32)
        return lo, hi

    def compute(t, slot):
        buf = bufs[slot]
        hrow = h_ref[pl.ds(t, 1), :]
        xn = _rms(hrow, g2_ref[...])
        acc = jnp.zeros((PEER_PICKS, LANES), F32)
        for c in range(half // LANES):
            u_lo, u_hi = unpack(buf[:, c * LANES:(c + 1) * LANES])
            acc = acc + (u_lo * xn[:, c * LANES:(c + 1) * LANES]
                         + u_hi * xn[:, half + c * LANES:half + (c + 1) * LANES])
        act = _gelu_tanh(jnp.sum(acc, axis=1, keepdims=True))
        gate = jnp.sum(jnp.where(lane == t, gate_ref[0], 0.0), axis=1, keepdims=True)
        w = jnp.broadcast_to(gate * act, (PEER_PICKS, LANES))
        mix_lo, mix_hi = [], []
        for c in range(half // LANES):
            v_lo, v_hi = unpack(buf[:, half + c * LANES:half + (c + 1) * LANES])
            mix_lo.append(jnp.sum(w * v_lo, axis=0, keepdims=True))
            mix_hi.append(jnp.sum(w * v_hi, axis=0, keepdims=True))
        o_ref[pl.ds(t, 1), :] = hrow + jnp.concatenate(mix_lo + mix_hi, axis=1)

    ahead = GATHER_DEPTH - TOKENS_PER_STEP

    def group(j, last):
        for b in range(0, GATHER_DEPTH, TOKENS_PER_STEP):
            t = j * GATHER_DEPTH + b
            for i in range(TOKENS_PER_STEP):
                wait_rows(b + i)
            for i in range(TOKENS_PER_STEP):
                if not last or b + i + ahead < GATHER_DEPTH:
                    start_rows(t + i + ahead, (b + i + ahead) % GATHER_DEPTH)
            for i in range(TOKENS_PER_STEP):
                compute(t + i, b + i)

    for b in range(ahead):
        start_rows(b, b)
    groups = PEER_TB // GATHER_DEPTH

    def body(j, carry):
        group(j, False)
        return carry

    lax.fori_loop(0, groups - 1, body, 0)
    group(groups - 1, True)


def pack_expert_tables(u, v):
    def pack(x):
        bits = lax.bitcast_convert_type(x.astype(jnp.bfloat16), jnp.uint16).astype(jnp.uint32)
        half = x.shape[1] // 2
        return bits[:, :half] | (bits[:, half:] << 16)

    return jnp.concatenate([pack(u), pack(v)], axis=1)[:, None, :]


def peer_gather(idx, gate, h, g2, uv):
    n, d = h.shape
    nblk = n // PEER_TB
    return pl.pallas_call(
        _peer_gather_kernel,
        grid=(nblk,),
        in_specs=[pl.BlockSpec(memory_space=pl.ANY),
                  pl.BlockSpec((1, PEER_PICKS, PEER_TB), lambda i: (i, 0, 0)),
                  pl.BlockSpec((PEER_TB, d), lambda i: (i, 0)),
                  pl.BlockSpec((1, d), lambda i: (0, 0)),
                  pl.BlockSpec(memory_space=pl.ANY)],
        out_specs=pl.BlockSpec((PEER_TB, d), lambda i: (i, 0)),
        out_shape=jax.ShapeDtypeStruct((n, d), F32),
        scratch_shapes=([pltpu.SMEM((PEER_TB * PEER_PICKS,), I32)]
                        + [pltpu.VMEM((PEER_PICKS, d), jnp.uint32)] * GATHER_DEPTH
                        + [pltpu.SemaphoreType.DMA((GATHER_DEPTH + 1,))]),
        compiler_params=_params(("arbitrary",)),
        name="peer_gather",
    )(idx, gate, h, g2.reshape(1, d), uv)


def _rel_bucket(rel):
    nb = NUM_BUCKETS // 2
    max_exact = nb // 2
    n = jnp.abs(rel)
    large = max_exact + (jnp.log(jnp.maximum(n, 1).astype(F32) / max_exact)
                         / math.log(MAX_DISTANCE / max_exact) * (nb - max_exact)).astype(I32)
    large = jnp.minimum(large, nb - 1)
    return jnp.where(rel > 0, nb, 0) + jnp.where(n < max_exact, n, large)


def _bias(rel, table):
    nq, nk = rel.shape
    return jnp.transpose(table[_rel_bucket(rel)].astype(F32), (2, 0, 1)).reshape(N_KV_HEADS, GROUP * nq, nk)


def kernel(x_prompt, x_sample, cache_k, cache_v, state_conv, rel_bias, norm1_g, w_in, attn_sink, conv_w, conv_b,
           conv_ln_g, conv_ln_b, out_norm_attn_g, out_norm_conv_g, w_out, norm2_g, peer_wq, peer_subkeys, peer_u,
           peer_v, final_norm_g):
    batch, seq, d = x_prompt.shape
    dec_batch, dec_seq, _ = x_sample.shape
    depth = w_in.shape[0]
    n_p, n_s = batch * seq, dec_batch * dec_seq
    n = n_p + n_s
    nblk = n // PEER_TB

    h = jnp.concatenate([x_prompt.reshape(n_p, d), x_sample.reshape(n_s, d)], axis=0)

    qi = jnp.arange(CHUNK, dtype=I32)
    bias_p = _bias((jnp.arange(BAND, dtype=I32) - WINDOW)[None, :] - qi[:, None], rel_bias)
    ti = jnp.arange(dec_seq, dtype=I32)
    kpos_s = jnp.concatenate([jnp.arange(WINDOW, dtype=I32) - WINDOW, ti])
    bias_s = _bias(kpos_s[None, :] - ti[:, None], rel_bias)

    outs = {k: [] for k in ("kp", "vp", "cp", "ks", "vs", "cs")}
    for l in range(depth):
        wl = w_in[l]
        o1 = ATTN_WIDTH
        o2 = o1 + KV_WIDTH
        o3 = o2 + KV_WIDTH
        w_perm = jnp.concatenate([wl[:, :o1], wl[:, o3:], wl[:, o1:o3]], axis=1).astype(BF16)
        z = rms_matmul(h, norm1_g[l], w_perm, tn=IN_WIDTH // 4)

        att_p = attn_prompt(z, bias_p, attn_sink[l], batch, seq)
        att_s = attn_sample(z, cache_k[l].reshape(dec_batch, WINDOW, KV_WIDTH),
                            cache_v[l].reshape(dec_batch, WINDOW, KV_WIDTH), bias_s, attn_sink[l],
                            n_p, dec_batch, dec_seq)
        conv_p, tail_p = conv_prompt(z, conv_w[l], conv_b[l], conv_ln_g[l], conv_ln_b[l], batch, seq)
        state = jnp.pad(state_conv[l], ((0, 0), (CONV_HALO - CONV_STATE, 0), (0, 0)))
        conv_s, tail_s = conv_sample(z, state, conv_w[l], conv_b[l], conv_ln_g[l], conv_ln_b[l],
                                     n_p, dec_batch, dec_seq)
        h = merge(att_p, att_s, conv_p, conv_s, out_norm_attn_g[l], out_norm_conv_g[l], w_out[l].astype(BF16), h)

        q = rms_matmul(h, norm2_g[l], peer_wq[l].astype(BF16), tn=1024)
        idx_t, gate = peer_topk(q, peer_subkeys[l])
        idx = jnp.transpose(idx_t, (0, 2, 1)).reshape(nblk, PEER_TB * PEER_PICKS)
        uv = pack_expert_tables(peer_u[l], peer_v[l])
        h = peer_gather(idx, gate, h, norm2_g[l], uv)

        kv_p = jnp.stack([lax.slice(z, ((b + 1) * seq - WINDOW, K_OFF), ((b + 1) * seq, IN_WIDTH))
                          for b in range(batch)])
        kv_s = lax.slice(z, (n_p, K_OFF), (n, IN_WIDTH)).reshape(dec_batch, dec_seq, 2 * KV_WIDTH)
        outs["kp"].append(kv_p[:, :, :KV_WIDTH].reshape(batch, WINDOW, N_KV_HEADS, HEAD_DIM))
        outs["vp"].append(kv_p[:, :, KV_WIDTH:].reshape(batch, WINDOW, N_KV_HEADS, HEAD_DIM))
        outs["cp"].append(tail_p[:, CONV_HALO - CONV_STATE:])
        outs["ks"].append(kv_s[:, :, :KV_WIDTH].reshape(dec_batch, dec_seq, N_KV_HEADS, HEAD_DIM))
        outs["vs"].append(kv_s[:, :, KV_WIDTH:].reshape(dec_batch, dec_seq, N_KV_HEADS, HEAD_DIM))
        outs["cs"].append(tail_s[:, CONV_HALO - CONV_STATE:])

    h = rms_rows(h, final_norm_g)
    y_prompt = h[:n_p].reshape(batch, seq, d)
    y_sample = h[n_p:].reshape(dec_batch, dec_seq, d)
    return (y_prompt, y_sample, jnp.stack(outs["kp"]), jnp.stack(outs["vp"]), jnp.stack(outs["cp"]),
            jnp.stack(outs["ks"]), jnp.stack(outs["vs"]), jnp.stack(outs["cs"]))
```

```python
import functools
import math

import jax
import jax.numpy as jnp
from jax import lax
from jax.experimental import pallas as pl
from jax.experimental.pallas import tpu as pltpu

F32 = jnp.float32
BF16 = jnp.bfloat16
I32 = jnp.int32

D_MODEL = 2048
CHUNK = 64
ATTN_WIDTH = 1024
CONV_CH = 1024
HEAD_DIM = 64
N_HEADS = 16
N_KV_HEADS = 4
GROUP = N_HEADS // N_KV_HEADS
KV_WIDTH = N_KV_HEADS * HEAD_DIM
WINDOW = 128
BAND = WINDOW + CHUNK
CONV_WIDTH = 31
CONV_STATE = CONV_WIDTH - 1
CONV_HALO = 32
NUM_BUCKETS = 32
MAX_DISTANCE = 128
PEER_HEADS = 8
PEER_NKEYS = 128
PEER_HALF = 128
PEER_TOPK = 16
PEER_PICKS = PEER_HEADS * PEER_TOPK
IN_WIDTH = ATTN_WIDTH + 2 * KV_WIDTH + 2 * CONV_CH
EPS = 1e-6
ATTN_SCALE = 1.0 / math.sqrt(HEAD_DIM)
NEG_INF = -1e30

Q_OFF = 0
A_OFF = ATTN_WIDTH
G_OFF = A_OFF + CONV_CH
K_OFF = G_OFF + CONV_CH
V_OFF = K_OFF + KV_WIDTH

VMEM_LIMIT = 56 * 1024 * 1024


def _params(sem, vmem=VMEM_LIMIT):
    return pltpu.CompilerParams(dimension_semantics=sem, vmem_limit_bytes=vmem)


def _rms(x, g):
    return x * lax.rsqrt(jnp.mean(x * x, axis=-1, keepdims=True) + EPS) * g


def _rms_matmul_kernel(x_ref, g_ref, w_ref, o_ref, xn_ref):
    @pl.when(pl.program_id(1) == 0)
    def _():
        xn_ref[...] = _rms(x_ref[...], g_ref[...]).astype(BF16)

    o_ref[...] = jnp.dot(xn_ref[...], w_ref[...], preferred_element_type=F32)


def rms_matmul(x, g, w, *, tm=512, tn):
    n, d = x.shape
    m = w.shape[1]
    return pl.pallas_call(
        _rms_matmul_kernel,
        grid=(n // tm, m // tn),
        in_specs=[pl.BlockSpec((tm, d), lambda i, j: (i, 0)),
                  pl.BlockSpec((1, d), lambda i, j: (0, 0)),
                  pl.BlockSpec((d, tn), lambda i, j: (0, j))],
        out_specs=pl.BlockSpec((tm, tn), lambda i, j: (i, j)),
        out_shape=jax.ShapeDtypeStruct((n, m), F32),
        scratch_shapes=[pltpu.VMEM((tm, d), BF16)],
        compiler_params=_params(("parallel", "arbitrary")),
        name="rms_matmul",
    )(x, g.reshape(1, d), w)


def _attn_heads(q, kk, vv, bias_ref, sink_ref, mask):
    tq = q.shape[0]
    row = lax.broadcasted_iota(I32, (GROUP * tq, 1), 0)
    scores, sinks = [], []
    for g in range(N_KV_HEADS):
        qg = jnp.concatenate([q[:, (g * GROUP + j) * HEAD_DIM:(g * GROUP + j + 1) * HEAD_DIM]
                              for j in range(GROUP)], axis=0)
        kh = kk[:, g * HEAD_DIM:(g + 1) * HEAD_DIM]
        s = lax.dot_general(qg, kh, (((1,), (1,)), ((), ())), preferred_element_type=F32)
        s = s * ATTN_SCALE + bias_ref[g]
        if mask is not None:
            s = jnp.where(mask, s, NEG_INF)
        scores.append(s)
        sk = jnp.full((GROUP * tq, 1), sink_ref[g * GROUP + GROUP - 1], F32)
        for j in range(GROUP - 2, -1, -1):
            sk = jnp.where(row < (j + 1) * tq, sink_ref[g * GROUP + j], sk)
        sinks.append(sk)
    probs = []
    for s, sk in zip(scores, sinks):
        m = jnp.maximum(jnp.max(s, axis=-1, keepdims=True), sk)
        p = jnp.exp(s - m)
        probs.append(p * (1.0 / (jnp.sum(p, axis=-1, keepdims=True) + jnp.exp(sk - m))))
    outs = []
    for g, p in enumerate(probs):
        o = jnp.dot(p, vv[:, g * HEAD_DIM:(g + 1) * HEAD_DIM], preferred_element_type=F32)
        outs += [o[j * tq:(j + 1) * tq, :] for j in range(GROUP)]
    return jnp.concatenate(outs, axis=1)


def _attn_prompt_kernel(q_ref, k0_ref, k1_ref, k2_ref, v0_ref, v1_ref, v2_ref, bias_ref, sink_ref, o_ref):
    c = pl.program_id(1)
    kk = jnp.concatenate([k0_ref[...], k1_ref[...], k2_ref[...]], axis=0)
    vv = jnp.concatenate([v0_ref[...], v1_ref[...], v2_ref[...]], axis=0)
    col = lax.broadcasted_iota(I32, (1, BAND), 1)
    mask = col + c * CHUNK - WINDOW >= 0
    o_ref[...] = _attn_heads(q_ref[...], kk, vv, bias_ref, sink_ref, mask)


def attn_prompt(z, bias, sink, batch, seq):
    n_chunks = seq // CHUNK
    kb, vb = K_OFF // KV_WIDTH, V_OFF // KV_WIDTH

    def band_spec(back, colblock):
        return pl.BlockSpec((CHUNK, KV_WIDTH),
                            lambda b, c: (b * n_chunks + jnp.maximum(c - back, 0), colblock))

    return pl.pallas_call(
        _attn_prompt_kernel,
        grid=(batch, n_chunks),
        in_specs=[pl.BlockSpec((CHUNK, ATTN_WIDTH), lambda b, c: (b * n_chunks + c, 0)),
                  band_spec(2, kb), band_spec(1, kb), band_spec(0, kb),
                  band_spec(2, vb), band_spec(1, vb), band_spec(0, vb),
                  pl.BlockSpec((N_KV_HEADS, GROUP * CHUNK, BAND), lambda b, c: (0, 0, 0)),
                  pl.BlockSpec(memory_space=pltpu.SMEM)],
        out_specs=pl.BlockSpec((CHUNK, ATTN_WIDTH), lambda b, c: (b * n_chunks + c, 0)),
        out_shape=jax.ShapeDtypeStruct((batch * seq, ATTN_WIDTH), F32),
        compiler_params=_params(("parallel", "arbitrary")),
        name="attn_prompt",
    )(z, z, z, z, z, z, z, bias, sink)


def _attn_sample_kernel(q_ref, kn_ref, vn_ref, kc_ref, vc_ref, bias_ref, sink_ref, o_ref):
    kk = jnp.concatenate([kc_ref[0], kn_ref[...]], axis=0)
    vv = jnp.concatenate([vc_ref[0], vn_ref[...]], axis=0)
    o_ref[...] = _attn_heads(q_ref[...], kk, vv, bias_ref, sink_ref, None)


def attn_sample(z, cache_k, cache_v, bias, sink, row0, dec_batch, dec_seq):
    r0 = row0 // dec_seq
    kb, vb = K_OFF // KV_WIDTH, V_OFF // KV_WIDTH
    return pl.pallas_call(
        _attn_sample_kernel,
        grid=(dec_batch,),
        in_specs=[pl.BlockSpec((dec_seq, ATTN_WIDTH), lambda b: (r0 + b, 0)),
                  pl.BlockSpec((dec_seq, KV_WIDTH), lambda b: (r0 + b, kb)),
                  pl.BlockSpec((dec_seq, KV_WIDTH), lambda b: (r0 + b, vb)),
                  pl.BlockSpec((1, WINDOW, KV_WIDTH), lambda b: (b, 0, 0)),
                  pl.BlockSpec((1, WINDOW, KV_WIDTH), lambda b: (b, 0, 0)),
                  pl.BlockSpec((N_KV_HEADS, GROUP * dec_seq, WINDOW + dec_seq), lambda b: (0, 0, 0)),
                  pl.BlockSpec(memory_space=pltpu.SMEM)],
        out_specs=pl.BlockSpec((dec_seq, ATTN_WIDTH), lambda b: (b, 0)),
        out_shape=jax.ShapeDtypeStruct((dec_batch * dec_seq, ATTN_WIDTH), F32),
        compiler_params=_params(("parallel",)),
        name="attn_sample",
    )(z, z, z, cache_k, cache_v, bias, sink)


CONV_ROWS = 32
SUBLANES = 8
CONV_SHIFT_PAD = CONV_HALO - SUBLANES


def _conv_kernel(pa_ref, pg_ref, a_ref, g_ref, w_ref, cb_ref, lg_ref, lb_ref, o_ref, tail_ref, ext_ref, sh_ref,
                 *, rows, prev_is_glu):
    if prev_is_glu:
        prev = pa_ref[0]
    else:
        prev = pa_ref[...] * jax.nn.sigmoid(pg_ref[...])
        prev = jnp.where(pl.program_id(1) == 0, 0.0, prev)
    ext_ref[0:CONV_HALO, :] = prev
    ext_ref[CONV_HALO:CONV_HALO + rows, :] = a_ref[...] * jax.nn.sigmoid(g_ref[...])
    span = rows + CONV_SHIFT_PAD
    for s in range(1, SUBLANES):
        sh_ref[s - 1] = ext_ref[s:s + span, :]
    step = min(CONV_ROWS, rows)
    for r0 in range(0, rows, step):
        acc = jnp.zeros((step, CONV_CH), F32)
        for j in range(CONV_WIDTH):
            off = r0 + j + CONV_HALO - CONV_STATE
            shift, base = off % SUBLANES, off - off % SUBLANES
            window = ext_ref[base:base + step, :] if shift == 0 else sh_ref[shift - 1, base:base + step, :]
            acc = acc + w_ref[j:j + 1, :] * window
        y = acc + cb_ref[...]
        mu = jnp.mean(y, axis=-1, keepdims=True)
        yc = y - mu
        yn = yc * lax.rsqrt(jnp.mean(yc * yc, axis=-1, keepdims=True) + EPS) * lg_ref[...] + lb_ref[...]
        o_ref[r0:r0 + step, :] = yn * jax.nn.sigmoid(yn)
    tail_ref[0] = ext_ref[rows:rows + CONV_HALO, :]


def _conv_call(kernel, grid, in_specs, out_specs, out_shape, rows, sem, name, args):
    return pl.pallas_call(
        kernel, grid=grid, in_specs=in_specs, out_specs=out_specs, out_shape=out_shape,
        scratch_shapes=[pltpu.VMEM((CONV_HALO + rows, CONV_CH), F32),
                        pltpu.VMEM((SUBLANES - 1, rows + CONV_SHIFT_PAD, CONV_CH), F32)],
        compiler_params=_params(sem), name=name,
    )(*args)


def _conv_param_specs(nd):
    zero = (lambda b, t: (0, 0)) if nd == 2 else (lambda b: (0, 0))
    return [pl.BlockSpec((CONV_WIDTH, CONV_CH), zero)] + [pl.BlockSpec((1, CONV_CH), zero)] * 3


def conv_prompt(z, w, cb, lg, lb, batch, seq, *, rows=256):
    tiles = seq // rows
    halo_per_tile = rows // CONV_HALO
    ab, gb = A_OFF // CONV_CH, G_OFF // CONV_CH

    def prev_spec(colblock):
        return pl.BlockSpec(
            (CONV_HALO, CONV_CH),
            lambda b, t: (jnp.maximum((b * tiles + t) * halo_per_tile - 1, 0), colblock))

    def cur_spec(colblock):
        return pl.BlockSpec((rows, CONV_CH), lambda b, t: (b * tiles + t, colblock))

    return _conv_call(
        functools.partial(_conv_kernel, rows=rows, prev_is_glu=False),
        (batch, tiles),
        [prev_spec(ab), prev_spec(gb), cur_spec(ab), cur_spec(gb)] + _conv_param_specs(2),
        [pl.BlockSpec((rows, CONV_CH), lambda b, t: (b * tiles + t, 0)),
         pl.BlockSpec((1, CONV_HALO, CONV_CH), lambda b, t: (b, 0, 0))],
        [jax.ShapeDtypeStruct((batch * seq, CONV_CH), F32),
         jax.ShapeDtypeStruct((batch, CONV_HALO, CONV_CH), F32)],
        rows, ("parallel", "arbitrary"), "conv_prompt",
        (z, z, z, z, w, cb.reshape(1, -1), lg.reshape(1, -1), lb.reshape(1, -1)))


def conv_sample(z, state, w, cb, lg, lb, row0, dec_batch, dec_seq):
    r0 = row0 // dec_seq
    ab, gb = A_OFF // CONV_CH, G_OFF // CONV_CH
    st_spec = pl.BlockSpec((1, CONV_HALO, CONV_CH), lambda b: (b, 0, 0))
    return _conv_call(
        functools.partial(_conv_kernel, rows=dec_seq, prev_is_glu=True),
        (dec_batch,),
        [st_spec, st_spec,
         pl.BlockSpec((dec_seq, CONV_CH), lambda b: (r0 + b, ab)),
         pl.BlockSpec((dec_seq, CONV_CH), lambda b: (r0 + b, gb))] + _conv_param_specs(1),
        [pl.BlockSpec((dec_seq, CONV_CH), lambda b: (b, 0)),
         pl.BlockSpec((1, CONV_HALO, CONV_CH), lambda b: (b, 0, 0))],
        [jax.ShapeDtypeStruct((dec_batch * dec_seq, CONV_CH), F32),
         jax.ShapeDtypeStruct((dec_batch, CONV_HALO, CONV_CH), F32)],
        dec_seq, ("parallel",), "conv_sample",
        (state, state, z, z, w, cb.reshape(1, -1), lg.reshape(1, -1), lb.reshape(1, -1)))


def _merge_kernel(ap_ref, as_ref, cp_ref, cs_ref, ga_ref, gc_ref, wa_ref, wc_ref, res_ref, o_ref, an_ref, cn_ref,
                  *, prompt_tiles):
    first = pl.program_id(1) == 0
    is_prompt = pl.program_id(0) < prompt_tiles

    def normalise(att_ref, conv_ref):
        an_ref[...] = _rms(att_ref[...], ga_ref[...]).astype(BF16)
        cn_ref[...] = _rms(conv_ref[...], gc_ref[...]).astype(BF16)

    pl.when(first & is_prompt)(lambda: normalise(ap_ref, cp_ref))
    pl.when(first & jnp.logical_not(is_prompt))(lambda: normalise(as_ref, cs_ref))
    o_ref[...] = (res_ref[...] + jnp.dot(an_ref[...], wa_ref[...], preferred_element_type=F32)
                  + jnp.dot(cn_ref[...], wc_ref[...], preferred_element_type=F32))


def merge(att_p, att_s, conv_p, conv_s, ga, gc, w_out, res, *, tm=512, tn=1024):
    n = res.shape[0]
    d = w_out.shape[1]
    prompt_tiles = att_p.shape[0] // tm

    def rows_p(i, j):
        return (jnp.minimum(i, prompt_tiles - 1), 0)

    def rows_s(i, j):
        return (jnp.maximum(i - prompt_tiles, 0), 0)

    return pl.pallas_call(
        functools.partial(_merge_kernel, prompt_tiles=prompt_tiles),
        grid=(n // tm, d // tn),
        in_specs=[pl.BlockSpec((tm, ATTN_WIDTH), rows_p),
                  pl.BlockSpec((tm, ATTN_WIDTH), rows_s),
                  pl.BlockSpec((tm, CONV_CH), rows_p),
                  pl.BlockSpec((tm, CONV_CH), rows_s),
                  pl.BlockSpec((1, ATTN_WIDTH), lambda i, j: (0, 0)),
                  pl.BlockSpec((1, CONV_CH), lambda i, j: (0, 0)),
                  pl.BlockSpec((ATTN_WIDTH, tn), lambda i, j: (0, j)),
                  pl.BlockSpec((CONV_CH, tn), lambda i, j: (1, j)),
                  pl.BlockSpec((tm, tn), lambda i, j: (i, j))],
        out_specs=pl.BlockSpec((tm, tn), lambda i, j: (i, j)),
        out_shape=jax.ShapeDtypeStruct((n, d), F32),
        scratch_shapes=[pltpu.VMEM((tm, ATTN_WIDTH), BF16), pltpu.VMEM((tm, CONV_CH), BF16)],
        compiler_params=_params(("parallel", "arbitrary")),
        name="merge",
    )(att_p, att_s, conv_p, conv_s, ga.reshape(1, -1), gc.reshape(1, -1), w_out, w_out, res)


def _final_norm_kernel(x_ref, g_ref, op_ref, os_ref, *, prompt_tiles):
    y = _rms(x_ref[...], g_ref[...])
    is_prompt = pl.program_id(0) < prompt_tiles

    @pl.when(is_prompt)
    def _():
        op_ref[...] = y

    @pl.when(jnp.logical_not(is_prompt))
    def _():
        os_ref[...] = y


def final_norm(x, g, n_p, *, tm=512):
    n, d = x.shape
    prompt_tiles = n_p // tm
    return pl.pallas_call(
        functools.partial(_final_norm_kernel, prompt_tiles=prompt_tiles),
        grid=(n // tm,),
        in_specs=[pl.BlockSpec((tm, d), lambda i: (i, 0)), pl.BlockSpec((1, d), lambda i: (0, 0))],
        out_specs=[pl.BlockSpec((tm, d), lambda i: (jnp.minimum(i, prompt_tiles - 1), 0)),
                   pl.BlockSpec((tm, d), lambda i: (jnp.maximum(i - prompt_tiles, 0), 0))],
        out_shape=[jax.ShapeDtypeStruct((n_p, d), F32), jax.ShapeDtypeStruct((n - n_p, d), F32)],
        compiler_params=_params(("arbitrary",)),
        name="final_norm",
    )(x, g.reshape(1, d))


PEER_TB = 128


def _topk_rows(chains, k, rank, payloads=None):
    chains = list(chains)
    vals = [[] for _ in chains]
    picks = [[] for _ in chains]
    for _ in range(k):
        for c, s in enumerate(chains):
            m = jnp.max(s, axis=0, keepdims=True)
            first = jnp.min(jnp.where(s == m, rank, RANK_NONE), axis=0, keepdims=True)
            hit = rank == first
            vals[c].append(m)
            picks[c].append(first if payloads is None
                            else jnp.max(jnp.where(hit, payloads[c], -1.0), axis=0, keepdims=True))
            chains[c] = jnp.where(hit, -jnp.inf, s)
    return ([jnp.concatenate(v, axis=0) for v in vals], [jnp.concatenate(p, axis=0) for p in picks])


RANK_NONE = 1e9

PAIR_SEGMENTS = ((0, 1, 0, 16), (1, 1, 0, 8), (2, 1, 0, 8), (3, 1, 0, 8),
                 (8, 8, 0, 1), (4, 4, 0, 1), (4, 4, 1, 1), (4, 1, 2, 1))
PAIR_ROWS = 64


def _pair_rows(a, b, combine):
    return [combine(a[i0:i0 + ni, :], b[j0:j0 + nj, :]) for i0, ni, j0, nj in PAIR_SEGMENTS]


def pair_ranks():
    rows = []
    for i0, ni, j0, nj in PAIR_SEGMENTS:
        rows += [float(i * PEER_TOPK + j) for i in range(i0, i0 + ni) for j in range(j0, j0 + nj)]
    rows += [RANK_NONE] * (PAIR_ROWS - len(rows))
    return jnp.broadcast_to(jnp.asarray(rows, F32)[:, None], (PAIR_ROWS, PEER_TB))


TOPK_HEADS = 2


def _peer_topk_kernel(q_ref, keys_ref, prank_ref, idx_ref, gate_ref):
    q = q_ref[...]
    dn = (((1,), (1,)), ((), ()))
    scores = [lax.dot_general(keys_ref[c % 2], q[:, c * PEER_HALF:(c + 1) * PEER_HALF], dn,
                              preferred_element_type=F32) for c in range(2 * TOPK_HEADS)]
    key_rank = lax.broadcasted_iota(I32, scores[0].shape, 0).astype(F32)
    vals, keys = _topk_rows(scores, PEER_TOPK, key_rank)
    pad = PAIR_ROWS - sum(ni * nj for _, ni, _, nj in PAIR_SEGMENTS)
    fill = jnp.full((pad, PEER_TB), -jnp.inf, F32)
    cand = [jnp.concatenate(_pair_rows(vals[2 * h], vals[2 * h + 1], lambda x, y: x + y) + [fill], axis=0)
            for h in range(TOPK_HEADS)]
    expert = [jnp.concatenate(_pair_rows(keys[2 * h], keys[2 * h + 1], lambda x, y: x * PEER_NKEYS + y) + [fill],
                              axis=0) for h in range(TOPK_HEADS)]
    cvs, picks = _topk_rows(cand, PEER_TOPK, prank_ref[...], payloads=expert)
    for h in range(TOPK_HEADS):
        e = jnp.exp(cvs[h] - cvs[h][0:1, :])
        idx_ref[0, h * PEER_TOPK:(h + 1) * PEER_TOPK, :] = picks[h].astype(I32)
        gate_ref[0, h * PEER_TOPK:(h + 1) * PEER_TOPK, :] = e / jnp.sum(e, axis=0, keepdims=True)


def peer_topk(q, subkeys):
    n = q.shape[0]
    nblk = n // PEER_TB
    out = jax.ShapeDtypeStruct((nblk, PEER_PICKS, PEER_TB), I32)
    return pl.pallas_call(
        _peer_topk_kernel,
        grid=(nblk, PEER_HEADS // TOPK_HEADS),
        in_specs=[pl.BlockSpec((PEER_TB, TOPK_HEADS * 2 * PEER_HALF), lambda i, h: (i, h)),
                  pl.BlockSpec((2, PEER_NKEYS, PEER_HALF), lambda i, h: (0, 0, 0)),
                  pl.BlockSpec((PAIR_ROWS, PEER_TB), lambda i, h: (0, 0))],
        out_specs=[pl.BlockSpec((1, TOPK_HEADS * PEER_TOPK, PEER_TB), lambda i, h: (i, h, 0))] * 2,
        out_shape=[out, jax.ShapeDtypeStruct(out.shape, F32)],
        compiler_params=_params(("parallel", "arbitrary")),
        name="peer_topk",
    )(q, subkeys, pair_ranks())


def _gelu_tanh(x):
    return 0.5 * x * (1.0 + jnp.tanh(math.sqrt(2.0 / math.pi) * (x + 0.044715 * x * x * x)))


GATHER_DEPTH = 8
TOKENS_PER_STEP = 2
LANES = 128


def _peer_gather_kernel(idx_hbm, gate_ref, h_ref, g2_ref, uv_hbm, o_ref, idx_ref, *scratch):
    bufs, sems = scratch[:GATHER_DEPTH], scratch[GATHER_DEPTH]
    blk = pl.program_id(0)
    idx_copy = pltpu.make_async_copy(idx_hbm.at[blk], idx_ref, sems.at[GATHER_DEPTH])
    idx_copy.start()
    idx_copy.wait()

    def start_rows(t, slot):
        base = t * PEER_PICKS
        for k in range(PEER_PICKS):
            e = idx_ref[base + k]
            pltpu.make_async_copy(uv_hbm.at[e], bufs[slot].at[pl.ds(k, 1)], sems.at[slot]).start(priority=k % 2)

    def wait_rows(slot):
        pltpu.make_async_copy(uv_hbm.at[pl.ds(0, PEER_PICKS), 0], bufs[slot], sems.at[slot]).wait()

    lane = lax.broadcasted_iota(I32, (PEER_PICKS, PEER_TB), 1)
    half = D_MODEL // 2

    def unpack(words):
        lo = lax.bitcast_convert_type(words << 16, F32)
        hi = lax.bitcast_convert_type(words & jnp.uint32(0xFFFF0000), F32)
        return lo, hi

    def compute(t, slot):
        buf = bufs[slot]
        hrow = h_ref[pl.ds(t, 1), :]
        xn = _rms(hrow, g2_ref[...])
        acc = jnp.zeros((PEER_PICKS, LANES), F32)
        for c in range(half // LANES):
            u_lo, u_hi = unpack(buf[:, c * LANES:(c + 1) * LANES])
            acc = acc + (u_lo * xn[:, c * LANES:(c + 1) * LANES]
                         + u_hi * xn[:, half + c * LANES:half + (c + 1) * LANES])
        act = _gelu_tanh(jnp.sum(acc, axis=1, keepdims=True))
        gate = jnp.sum(jnp.where(lane == t, gate_ref[0], 0.0), axis=1, keepdims=True)
        w = jnp.broadcast_to(gate * act, (PEER_PICKS, LANES))
        mix_lo, mix_hi = [], []
        for c in range(half // LANES):
            v_lo, v_hi = unpack(buf[:, half + c * LANES:half + (c + 1) * LANES])
            mix_lo.append(jnp.sum(w * v_lo, axis=0, keepdims=True))
            mix_hi.append(jnp.sum(w * v_hi, axis=0, keepdims=True))
        o_ref[pl.ds(t, 1), :] = hrow + jnp.concatenate(mix_lo + mix_hi, axis=1)

    ahead = GATHER_DEPTH - TOKENS_PER_STEP

    def group(j, last):
        for b in range(0, GATHER_DEPTH, TOKENS_PER_STEP):
            t = j * GATHER_DEPTH + b
            for i in range(TOKENS_PER_STEP):
                wait_rows(b + i)
            for i in range(TOKENS_PER_STEP):
                if not last or b + i + ahead < GATHER_DEPTH:
                    start_rows(t + i + ahead, (b + i + ahead) % GATHER_DEPTH)
            for i in range(TOKENS_PER_STEP):
                compute(t + i, b + i)

    for b in range(ahead):
        start_rows(b, b)
    groups = PEER_TB // GATHER_DEPTH

    def body(j, carry):
        group(j, False)
        return carry

    lax.fori_loop(0, groups - 1, body, 0)
    group(groups - 1, True)


def _bf16_bits_rounded(x):
    bits = lax.bitcast_convert_type(x, jnp.uint32)
    return bits + jnp.uint32(0x7FFF) + ((bits >> 16) & jnp.uint32(1))


def _pack_kernel(u_ref, v_ref, o_ref):
    half = u_ref.shape[1] // 2
    for src, base in ((u_ref, 0), (v_ref, half)):
        r = _bf16_bits_rounded(src[...])
        o_ref[:, 0, base:base + half] = (r[:, :half] >> 16) | (r[:, half:] & jnp.uint32(0xFFFF0000))


def pack_expert_tables(u, v, *, tm=256):
    e, d = u.shape
    return pl.pallas_call(
        _pack_kernel,
        grid=(e // tm,),
        in_specs=[pl.BlockSpec((tm, d), lambda i: (i, 0))] * 2,
        out_specs=pl.BlockSpec((tm, 1, d), lambda i: (i, 0, 0)),
        out_shape=jax.ShapeDtypeStruct((e, 1, d), jnp.uint32),
        compiler_params=_params(("parallel",)),
        name="pack_expert_tables",
    )(u, v)


def peer_gather(idx, gate, h, g2, uv):
    n, d = h.shape
    nblk = n // PEER_TB
    return pl.pallas_call(
        _peer_gather_kernel,
        grid=(nblk,),
        in_specs=[pl.BlockSpec(memory_space=pl.ANY),
                  pl.BlockSpec((1, PEER_PICKS, PEER_TB), lambda i: (i, 0, 0)),
                  pl.BlockSpec((PEER_TB, d), lambda i: (i, 0)),
                  pl.BlockSpec((1, d), lambda i: (0, 0)),
                  pl.BlockSpec(memory_space=pl.ANY)],
        out_specs=pl.BlockSpec((PEER_TB, d), lambda i: (i, 0)),
        out_shape=jax.ShapeDtypeStruct((n, d), F32),
        scratch_shapes=([pltpu.SMEM((PEER_TB * PEER_PICKS,), I32)]
                        + [pltpu.VMEM((PEER_PICKS, d), jnp.uint32)] * GATHER_DEPTH
                        + [pltpu.SemaphoreType.DMA((GATHER_DEPTH + 1,))]),
        compiler_params=_params(("arbitrary",)),
        name="peer_gather",
    )(idx, gate, h, g2.reshape(1, d), uv)


def _rel_bucket(rel):
    nb = NUM_BUCKETS // 2
    max_exact = nb // 2
    n = jnp.abs(rel)
    large = max_exact + (jnp.log(jnp.maximum(n, 1).astype(F32) / max_exact)
                         / math.log(MAX_DISTANCE / max_exact) * (nb - max_exact)).astype(I32)
    large = jnp.minimum(large, nb - 1)
    return jnp.where(rel > 0, nb, 0) + jnp.where(n < max_exact, n, large)


def _bias(rel, table):
    nq, nk = rel.shape
    return jnp.transpose(table[_rel_bucket(rel)].astype(F32), (2, 0, 1)).reshape(N_KV_HEADS, GROUP * nq, nk)


def kernel(x_prompt, x_sample, cache_k, cache_v, state_conv, rel_bias, norm1_g, w_in, attn_sink, conv_w, conv_b,
           conv_ln_g, conv_ln_b, out_norm_attn_g, out_norm_conv_g, w_out, norm2_g, peer_wq, peer_subkeys, peer_u,
           peer_v, final_norm_g):
    batch, seq, d = x_prompt.shape
    dec_batch, dec_seq, _ = x_sample.shape
    depth = w_in.shape[0]
    n_p, n_s = batch * seq, dec_batch * dec_seq
    n = n_p + n_s
    nblk = n // PEER_TB

    h = jnp.concatenate([x_prompt.reshape(n_p, d), x_sample.reshape(n_s, d)], axis=0)

    qi = jnp.arange(CHUNK, dtype=I32)
    bias_p = _bias((jnp.arange(BAND, dtype=I32) - WINDOW)[None, :] - qi[:, None], rel_bias)
    ti = jnp.arange(dec_seq, dtype=I32)
    kpos_s = jnp.concatenate([jnp.arange(WINDOW, dtype=I32) - WINDOW, ti])
    bias_s = _bias(kpos_s[None, :] - ti[:, None], rel_bias)

    outs = {k: [] for k in ("kp", "vp", "cp", "ks", "vs", "cs")}
    for l in range(depth):
        wl = w_in[l]
        o1 = ATTN_WIDTH
        o2 = o1 + KV_WIDTH
        o3 = o2 + KV_WIDTH
        w_perm = jnp.concatenate([wl[:, :o1], wl[:, o3:], wl[:, o1:o3]], axis=1).astype(BF16)
        z = rms_matmul(h, norm1_g[l], w_perm, tn=IN_WIDTH // 4)

        att_p = attn_prompt(z, bias_p, attn_sink[l], batch, seq)
        att_s = attn_sample(z, cache_k[l].reshape(dec_batch, WINDOW, KV_WIDTH),
                            cache_v[l].reshape(dec_batch, WINDOW, KV_WIDTH), bias_s, attn_sink[l],
                            n_p, dec_batch, dec_seq)
        conv_p, tail_p = conv_prompt(z, conv_w[l], conv_b[l], conv_ln_g[l], conv_ln_b[l], batch, seq)
        state = jnp.pad(state_conv[l], ((0, 0), (CONV_HALO - CONV_STATE, 0), (0, 0)))
        conv_s, tail_s = conv_sample(z, state, conv_w[l], conv_b[l], conv_ln_g[l], conv_ln_b[l],
                                     n_p, dec_batch, dec_seq)
        h = merge(att_p, att_s, conv_p, conv_s, out_norm_attn_g[l], out_norm_conv_g[l], w_out[l].astype(BF16), h)

        q = rms_matmul(h, norm2_g[l], peer_wq[l].astype(BF16), tn=1024)
        idx_t, gate = peer_topk(q, peer_subkeys[l])
        idx = jnp.transpose(idx_t, (0, 2, 1)).reshape(nblk, PEER_TB * PEER_PICKS)
        uv = pack_expert_tables(peer_u[l], peer_v[l])
        h = peer_gather(idx, gate, h, norm2_g[l], uv)

        kv_p = jnp.stack([lax.slice(z, ((b + 1) * seq - WINDOW, K_OFF), ((b + 1) * seq, IN_WIDTH))
                          for b in range(batch)])
        kv_s = lax.slice(z, (n_p, K_OFF), (n, IN_WIDTH)).reshape(dec_batch, dec_seq, 2 * KV_WIDTH)
        outs["kp"].append(kv_p[:, :, :KV_WIDTH].reshape(batch, WINDOW, N_KV_HEADS, HEAD_DIM))
        outs["vp"].append(kv_p[:, :, KV_WIDTH:].reshape(batch, WINDOW, N_KV_HEADS, HEAD_DIM))
        outs["cp"].append(tail_p[:, CONV_HALO - CONV_STATE:])
        outs["ks"].append(kv_s[:, :, :KV_WIDTH].reshape(dec_batch, dec_seq, N_KV_HEADS, HEAD_DIM))
        outs["vs"].append(kv_s[:, :, KV_WIDTH:].reshape(dec_batch, dec_seq, N_KV_HEADS, HEAD_DIM))
        outs["cs"].append(tail_s[:, CONV_HALO - CONV_STATE:])

    y_p, y_s = final_norm(h, final_norm_g, n_p)
    y_prompt = y_p.reshape(batch, seq, d)
    y_sample = y_s.reshape(dec_batch, dec_seq, d)
    return (y_prompt, y_sample, jnp.stack(outs["kp"]), jnp.stack(outs["vp"]), jnp.stack(outs["cp"]),
            jnp.stack(outs["ks"]), jnp.stack(outs["vs"]), jnp.stack(outs["cs"]))
```

```python
import functools
import math

import jax
import jax.numpy as jnp
from jax import lax
from jax.experimental import pallas as pl
from jax.experimental.pallas import tpu as pltpu

F32 = jnp.float32
BF16 = jnp.bfloat16
I32 = jnp.int32

D_MODEL = 2048
CHUNK = 64
ATTN_WIDTH = 1024
CONV_CH = 1024
HEAD_DIM = 64
N_HEADS = 16
N_KV_HEADS = 4
GROUP = N_HEADS // N_KV_HEADS
KV_WIDTH = N_KV_HEADS * HEAD_DIM
WINDOW = 128
BAND = WINDOW + CHUNK
CONV_WIDTH = 31
CONV_STATE = CONV_WIDTH - 1
CONV_HALO = 32
NUM_BUCKETS = 32
MAX_DISTANCE = 128
PEER_HEADS = 8
PEER_NKEYS = 128
PEER_HALF = 128
PEER_TOPK = 16
PEER_PICKS = PEER_HEADS * PEER_TOPK
IN_WIDTH = ATTN_WIDTH + 2 * KV_WIDTH + 2 * CONV_CH
EPS = 1e-6
ATTN_SCALE = 1.0 / math.sqrt(HEAD_DIM)
NEG_INF = -1e30

Q_OFF = 0
A_OFF = ATTN_WIDTH
G_OFF = A_OFF + CONV_CH
K_OFF = G_OFF + CONV_CH
V_OFF = K_OFF + KV_WIDTH

VMEM_LIMIT = 56 * 1024 * 1024


def _params(sem, vmem=VMEM_LIMIT):
    return pltpu.CompilerParams(dimension_semantics=sem, vmem_limit_bytes=vmem)


def _rms(x, g):
    return x * lax.rsqrt(jnp.mean(x * x, axis=-1, keepdims=True) + EPS) * g


def _rms_matmul_kernel(x_ref, g_ref, w_ref, o_ref, xn_ref):
    @pl.when(pl.program_id(1) == 0)
    def _():
        xn_ref[...] = _rms(x_ref[...], g_ref[...]).astype(BF16)

    o_ref[...] = jnp.dot(xn_ref[...], w_ref[...], preferred_element_type=F32)


def rms_matmul(x, g, w, *, tm=512, tn):
    n, d = x.shape
    m = w.shape[1]
    return pl.pallas_call(
        _rms_matmul_kernel,
        grid=(n // tm, m // tn),
        in_specs=[pl.BlockSpec((tm, d), lambda i, j: (i, 0)),
                  pl.BlockSpec((1, d), lambda i, j: (0, 0)),
                  pl.BlockSpec((d, tn), lambda i, j: (0, j))],
        out_specs=pl.BlockSpec((tm, tn), lambda i, j: (i, j)),
        out_shape=jax.ShapeDtypeStruct((n, m), F32),
        scratch_shapes=[pltpu.VMEM((tm, d), BF16)],
        compiler_params=_params(("parallel", "arbitrary")),
        name="rms_matmul",
    )(x, g.reshape(1, d), w)


def _attn_heads(q, kk, vv, bias_ref, sink_ref, mask):
    tq = q.shape[0]
    row = lax.broadcasted_iota(I32, (GROUP * tq, 1), 0)
    scores, sinks = [], []
    for g in range(N_KV_HEADS):
        qg = jnp.concatenate([q[:, (g * GROUP + j) * HEAD_DIM:(g * GROUP + j + 1) * HEAD_DIM]
                              for j in range(GROUP)], axis=0)
        kh = kk[:, g * HEAD_DIM:(g + 1) * HEAD_DIM]
        s = lax.dot_general(qg, kh, (((1,), (1,)), ((), ())), preferred_element_type=F32)
        s = s * ATTN_SCALE + bias_ref[g]
        if mask is not None:
            s = jnp.where(mask, s, NEG_INF)
        scores.append(s)
        sk = jnp.full((GROUP * tq, 1), sink_ref[g * GROUP + GROUP - 1], F32)
        for j in range(GROUP - 2, -1, -1):
            sk = jnp.where(row < (j + 1) * tq, sink_ref[g * GROUP + j], sk)
        sinks.append(sk)
    probs = []
    for s, sk in zip(scores, sinks):
        m = jnp.maximum(jnp.max(s, axis=-1, keepdims=True), sk)
        p = jnp.exp(s - m)
        probs.append(p * (1.0 / (jnp.sum(p, axis=-1, keepdims=True) + jnp.exp(sk - m))))
    outs = []
    for g, p in enumerate(probs):
        o = jnp.dot(p, vv[:, g * HEAD_DIM:(g + 1) * HEAD_DIM], preferred_element_type=F32)
        outs += [o[j * tq:(j + 1) * tq, :] for j in range(GROUP)]
    return jnp.concatenate(outs, axis=1)


def _attn_prompt_kernel(q_ref, k0_ref, k1_ref, k2_ref, v0_ref, v1_ref, v2_ref, bias_ref, sink_ref, o_ref):
    c = pl.program_id(1)
    kk = jnp.concatenate([k0_ref[...], k1_ref[...], k2_ref[...]], axis=0)
    vv = jnp.concatenate([v0_ref[...], v1_ref[...], v2_ref[...]], axis=0)
    col = lax.broadcasted_iota(I32, (1, BAND), 1)
    mask = col + c * CHUNK - WINDOW >= 0
    o_ref[...] = _attn_heads(q_ref[...], kk, vv, bias_ref, sink_ref, mask)


def attn_prompt(z, bias, sink, batch, seq):
    n_chunks = seq // CHUNK
    kb, vb = K_OFF // KV_WIDTH, V_OFF // KV_WIDTH

    def band_spec(back, colblock):
        return pl.BlockSpec((CHUNK, KV_WIDTH),
                            lambda b, c: (b * n_chunks + jnp.maximum(c - back, 0), colblock))

    return pl.pallas_call(
        _attn_prompt_kernel,
        grid=(batch, n_chunks),
        in_specs=[pl.BlockSpec((CHUNK, ATTN_WIDTH), lambda b, c: (b * n_chunks + c, 0)),
                  band_spec(2, kb), band_spec(1, kb), band_spec(0, kb),
                  band_spec(2, vb), band_spec(1, vb), band_spec(0, vb),
                  pl.BlockSpec((N_KV_HEADS, GROUP * CHUNK, BAND), lambda b, c: (0, 0, 0)),
                  pl.BlockSpec(memory_space=pltpu.SMEM)],
        out_specs=pl.BlockSpec((CHUNK, ATTN_WIDTH), lambda b, c: (b * n_chunks + c, 0)),
        out_shape=jax.ShapeDtypeStruct((batch * seq, ATTN_WIDTH), F32),
        compiler_params=_params(("parallel", "arbitrary")),
        name="attn_prompt",
    )(z, z, z, z, z, z, z, bias, sink)


def _attn_sample_kernel(q_ref, kn_ref, vn_ref, kc_ref, vc_ref, bias_ref, sink_ref, o_ref):
    kk = jnp.concatenate([kc_ref[0], kn_ref[...]], axis=0)
    vv = jnp.concatenate([vc_ref[0], vn_ref[...]], axis=0)
    o_ref[...] = _attn_heads(q_ref[...], kk, vv, bias_ref, sink_ref, None)


def attn_sample(z, cache_k, cache_v, bias, sink, row0, dec_batch, dec_seq):
    r0 = row0 // dec_seq
    kb, vb = K_OFF // KV_WIDTH, V_OFF // KV_WIDTH
    return pl.pallas_call(
        _attn_sample_kernel,
        grid=(dec_batch,),
        in_specs=[pl.BlockSpec((dec_seq, ATTN_WIDTH), lambda b: (r0 + b, 0)),
                  pl.BlockSpec((dec_seq, KV_WIDTH), lambda b: (r0 + b, kb)),
                  pl.BlockSpec((dec_seq, KV_WIDTH), lambda b: (r0 + b, vb)),
                  pl.BlockSpec((1, WINDOW, KV_WIDTH), lambda b: (b, 0, 0)),
                  pl.BlockSpec((1, WINDOW, KV_WIDTH), lambda b: (b, 0, 0)),
                  pl.BlockSpec((N_KV_HEADS, GROUP * dec_seq, WINDOW + dec_seq), lambda b: (0, 0, 0)),
                  pl.BlockSpec(memory_space=pltpu.SMEM)],
        out_specs=pl.BlockSpec((dec_seq, ATTN_WIDTH), lambda b: (b, 0)),
        out_shape=jax.ShapeDtypeStruct((dec_batch * dec_seq, ATTN_WIDTH), F32),
        compiler_params=_params(("parallel",)),
        name="attn_sample",
    )(z, z, z, cache_k, cache_v, bias, sink)


CONV_ROWS = 32
SUBLANES = 8
CONV_SHIFT_PAD = CONV_HALO - SUBLANES


def _conv_kernel(pa_ref, pg_ref, a_ref, g_ref, w_ref, cb_ref, lg_ref, lb_ref, o_ref, tail_ref, ext_ref, sh_ref,
                 *, rows, prev_is_glu):
    if prev_is_glu:
        prev = pa_ref[0]
    else:
        prev = pa_ref[...] * jax.nn.sigmoid(pg_ref[...])
        prev = jnp.where(pl.program_id(1) == 0, 0.0, prev)
    ext_ref[0:CONV_HALO, :] = prev
    ext_ref[CONV_HALO:CONV_HALO + rows, :] = a_ref[...] * jax.nn.sigmoid(g_ref[...])
    span = rows + CONV_SHIFT_PAD
    for s in range(1, SUBLANES):
        sh_ref[s - 1] = ext_ref[s:s + span, :]
    step = min(CONV_ROWS, rows)
    for r0 in range(0, rows, step):
        acc = jnp.zeros((step, CONV_CH), F32)
        for j in range(CONV_WIDTH):
            off = r0 + j + CONV_HALO - CONV_STATE
            shift, base = off % SUBLANES, off - off % SUBLANES
            window = ext_ref[base:base + step, :] if shift == 0 else sh_ref[shift - 1, base:base + step, :]
            acc = acc + w_ref[j:j + 1, :] * window
        y = acc + cb_ref[...]
        mu = jnp.mean(y, axis=-1, keepdims=True)
        yc = y - mu
        yn = yc * lax.rsqrt(jnp.mean(yc * yc, axis=-1, keepdims=True) + EPS) * lg_ref[...] + lb_ref[...]
        o_ref[r0:r0 + step, :] = yn * jax.nn.sigmoid(yn)
    tail_ref[0] = ext_ref[rows:rows + CONV_HALO, :]


def _conv_call(kernel, grid, in_specs, out_specs, out_shape, rows, sem, name, args):
    return pl.pallas_call(
        kernel, grid=grid, in_specs=in_specs, out_specs=out_specs, out_shape=out_shape,
        scratch_shapes=[pltpu.VMEM((CONV_HALO + rows, CONV_CH), F32),
                        pltpu.VMEM((SUBLANES - 1, rows + CONV_SHIFT_PAD, CONV_CH), F32)],
        compiler_params=_params(sem), name=name,
    )(*args)


def _conv_param_specs(nd):
    zero = (lambda b, t: (0, 0)) if nd == 2 else (lambda b: (0, 0))
    return [pl.BlockSpec((CONV_WIDTH, CONV_CH), zero)] + [pl.BlockSpec((1, CONV_CH), zero)] * 3


def conv_prompt(z, w, cb, lg, lb, batch, seq, *, rows=256):
    tiles = seq // rows
    halo_per_tile = rows // CONV_HALO
    ab, gb = A_OFF // CONV_CH, G_OFF // CONV_CH

    def prev_spec(colblock):
        return pl.BlockSpec(
            (CONV_HALO, CONV_CH),
            lambda b, t: (jnp.maximum((b * tiles + t) * halo_per_tile - 1, 0), colblock))

    def cur_spec(colblock):
        return pl.BlockSpec((rows, CONV_CH), lambda b, t: (b * tiles + t, colblock))

    return _conv_call(
        functools.partial(_conv_kernel, rows=rows, prev_is_glu=False),
        (batch, tiles),
        [prev_spec(ab), prev_spec(gb), cur_spec(ab), cur_spec(gb)] + _conv_param_specs(2),
        [pl.BlockSpec((rows, CONV_CH), lambda b, t: (b * tiles + t, 0)),
         pl.BlockSpec((1, CONV_HALO, CONV_CH), lambda b, t: (b, 0, 0))],
        [jax.ShapeDtypeStruct((batch * seq, CONV_CH), F32),
         jax.ShapeDtypeStruct((batch, CONV_HALO, CONV_CH), F32)],
        rows, ("parallel", "arbitrary"), "conv_prompt",
        (z, z, z, z, w, cb.reshape(1, -1), lg.reshape(1, -1), lb.reshape(1, -1)))


def conv_sample(z, state, w, cb, lg, lb, row0, dec_batch, dec_seq):
    r0 = row0 // dec_seq
    ab, gb = A_OFF // CONV_CH, G_OFF // CONV_CH
    st_spec = pl.BlockSpec((1, CONV_HALO, CONV_CH), lambda b: (b, 0, 0))
    return _conv_call(
        functools.partial(_conv_kernel, rows=dec_seq, prev_is_glu=True),
        (dec_batch,),
        [st_spec, st_spec,
         pl.BlockSpec((dec_seq, CONV_CH), lambda b: (r0 + b, ab)),
         pl.BlockSpec((dec_seq, CONV_CH), lambda b: (r0 + b, gb))] + _conv_param_specs(1),
        [pl.BlockSpec((dec_seq, CONV_CH), lambda b: (b, 0)),
         pl.BlockSpec((1, CONV_HALO, CONV_CH), lambda b: (b, 0, 0))],
        [jax.ShapeDtypeStruct((dec_batch * dec_seq, CONV_CH), F32),
         jax.ShapeDtypeStruct((dec_batch, CONV_HALO, CONV_CH), F32)],
        dec_seq, ("parallel",), "conv_sample",
        (state, state, z, z, w, cb.reshape(1, -1), lg.reshape(1, -1), lb.reshape(1, -1)))


def _merge_kernel(ap_ref, as_ref, cp_ref, cs_ref, ga_ref, gc_ref, wa_ref, wc_ref, res_ref, o_ref, an_ref, cn_ref,
                  *, prompt_tiles):
    first = pl.program_id(1) == 0
    is_prompt = pl.program_id(0) < prompt_tiles

    def normalise(att_ref, conv_ref):
        an_ref[...] = _rms(att_ref[...], ga_ref[...]).astype(BF16)
        cn_ref[...] = _rms(conv_ref[...], gc_ref[...]).astype(BF16)

    pl.when(first & is_prompt)(lambda: normalise(ap_ref, cp_ref))
    pl.when(first & jnp.logical_not(is_prompt))(lambda: normalise(as_ref, cs_ref))
    o_ref[...] = (res_ref[...] + jnp.dot(an_ref[...], wa_ref[...], preferred_element_type=F32)
                  + jnp.dot(cn_ref[...], wc_ref[...], preferred_element_type=F32))


def merge(att_p, att_s, conv_p, conv_s, ga, gc, w_out, res, *, tm=512, tn=1024):
    n = res.shape[0]
    d = w_out.shape[1]
    prompt_tiles = att_p.shape[0] // tm

    def rows_p(i, j):
        return (jnp.minimum(i, prompt_tiles - 1), 0)

    def rows_s(i, j):
        return (jnp.maximum(i - prompt_tiles, 0), 0)

    return pl.pallas_call(
        functools.partial(_merge_kernel, prompt_tiles=prompt_tiles),
        grid=(n // tm, d // tn),
        in_specs=[pl.BlockSpec((tm, ATTN_WIDTH), rows_p),
                  pl.BlockSpec((tm, ATTN_WIDTH), rows_s),
                  pl.BlockSpec((tm, CONV_CH), rows_p),
                  pl.BlockSpec((tm, CONV_CH), rows_s),
                  pl.BlockSpec((1, ATTN_WIDTH), lambda i, j: (0, 0)),
                  pl.BlockSpec((1, CONV_CH), lambda i, j: (0, 0)),
                  pl.BlockSpec((ATTN_WIDTH, tn), lambda i, j: (0, j)),
                  pl.BlockSpec((CONV_CH, tn), lambda i, j: (1, j)),
                  pl.BlockSpec((tm, tn), lambda i, j: (i, j))],
        out_specs=pl.BlockSpec((tm, tn), lambda i, j: (i, j)),
        out_shape=jax.ShapeDtypeStruct((n, d), F32),
        scratch_shapes=[pltpu.VMEM((tm, ATTN_WIDTH), BF16), pltpu.VMEM((tm, CONV_CH), BF16)],
        compiler_params=_params(("parallel", "arbitrary")),
        name="merge",
    )(att_p, att_s, conv_p, conv_s, ga.reshape(1, -1), gc.reshape(1, -1), w_out, w_out, res)


def _final_norm_kernel(x_ref, g_ref, op_ref, os_ref, *, prompt_tiles):
    y = _rms(x_ref[...], g_ref[...])
    is_prompt = pl.program_id(0) < prompt_tiles

    @pl.when(is_prompt)
    def _():
        op_ref[...] = y

    @pl.when(jnp.logical_not(is_prompt))
    def _():
        os_ref[...] = y


def final_norm(x, g, n_p, *, tm=512):
    n, d = x.shape
    prompt_tiles = n_p // tm
    return pl.pallas_call(
        functools.partial(_final_norm_kernel, prompt_tiles=prompt_tiles),
        grid=(n // tm,),
        in_specs=[pl.BlockSpec((tm, d), lambda i: (i, 0)), pl.BlockSpec((1, d), lambda i: (0, 0))],
        out_specs=[pl.BlockSpec((tm, d), lambda i: (jnp.minimum(i, prompt_tiles - 1), 0)),
                   pl.BlockSpec((tm, d), lambda i: (jnp.maximum(i - prompt_tiles, 0), 0))],
        out_shape=[jax.ShapeDtypeStruct((n_p, d), F32), jax.ShapeDtypeStruct((n - n_p, d), F32)],
        compiler_params=_params(("arbitrary",)),
        name="final_norm",
    )(x, g.reshape(1, d))


PEER_TB = 128


def _topk_rows(chains, k, rank, payloads=None):
    chains = list(chains)
    vals = [[] for _ in chains]
    picks = [[] for _ in chains]
    for _ in range(k):
        for c, s in enumerate(chains):
            m = jnp.max(s, axis=0, keepdims=True)
            first = jnp.min(jnp.where(s == m, rank, RANK_NONE), axis=0, keepdims=True)
            hit = rank == first
            vals[c].append(m)
            picks[c].append(first if payloads is None
                            else jnp.max(jnp.where(hit, payloads[c], -1.0), axis=0, keepdims=True))
            chains[c] = jnp.where(hit, -jnp.inf, s)
    return ([jnp.concatenate(v, axis=0) for v in vals], [jnp.concatenate(p, axis=0) for p in picks])


RANK_NONE = 1e9

PAIR_SEGMENTS = ((0, 1, 0, 16), (1, 1, 0, 8), (2, 1, 0, 8), (3, 1, 0, 8),
                 (8, 8, 0, 1), (4, 4, 0, 1), (4, 4, 1, 1), (4, 1, 2, 1))
PAIR_ROWS = 64


def _pair_rows(a, b, combine):
    return [combine(a[i0:i0 + ni, :], b[j0:j0 + nj, :]) for i0, ni, j0, nj in PAIR_SEGMENTS]


def pair_ranks():
    rows = []
    for i0, ni, j0, nj in PAIR_SEGMENTS:
        rows += [float(i * PEER_TOPK + j) for i in range(i0, i0 + ni) for j in range(j0, j0 + nj)]
    rows += [RANK_NONE] * (PAIR_ROWS - len(rows))
    return jnp.broadcast_to(jnp.asarray(rows, F32)[:, None], (PAIR_ROWS, PEER_TB))


TOPK_HEADS = 2


def _peer_topk_kernel(q_ref, keys_ref, prank_ref, idx_ref, gate_ref):
    q = q_ref[...]
    dn = (((1,), (1,)), ((), ()))
    scores = [lax.dot_general(keys_ref[c % 2], q[:, c * PEER_HALF:(c + 1) * PEER_HALF], dn,
                              preferred_element_type=F32) for c in range(2 * TOPK_HEADS)]
    key_rank = lax.broadcasted_iota(I32, scores[0].shape, 0).astype(F32)
    vals, keys = _topk_rows(scores, PEER_TOPK, key_rank)
    pad = PAIR_ROWS - sum(ni * nj for _, ni, _, nj in PAIR_SEGMENTS)
    fill = jnp.full((pad, PEER_TB), -jnp.inf, F32)
    cand = [jnp.concatenate(_pair_rows(vals[2 * h], vals[2 * h + 1], lambda x, y: x + y) + [fill], axis=0)
            for h in range(TOPK_HEADS)]
    expert = [jnp.concatenate(_pair_rows(keys[2 * h], keys[2 * h + 1], lambda x, y: x * PEER_NKEYS + y) + [fill],
                              axis=0) for h in range(TOPK_HEADS)]
    cvs, picks = _topk_rows(cand, PEER_TOPK, prank_ref[...], payloads=expert)
    for h in range(TOPK_HEADS):
        e = jnp.exp(cvs[h] - cvs[h][0:1, :])
        idx_ref[0, h * PEER_TOPK:(h + 1) * PEER_TOPK, :] = picks[h].astype(I32)
        gate_ref[0, h * PEER_TOPK:(h + 1) * PEER_TOPK, :] = e / jnp.sum(e, axis=0, keepdims=True)


def peer_topk(q, subkeys):
    n = q.shape[0]
    nblk = n // PEER_TB
    out = jax.ShapeDtypeStruct((nblk, PEER_PICKS, PEER_TB), I32)
    return pl.pallas_call(
        _peer_topk_kernel,
        grid=(nblk, PEER_HEADS // TOPK_HEADS),
        in_specs=[pl.BlockSpec((PEER_TB, TOPK_HEADS * 2 * PEER_HALF), lambda i, h: (i, h)),
                  pl.BlockSpec((2, PEER_NKEYS, PEER_HALF), lambda i, h: (0, 0, 0)),
                  pl.BlockSpec((PAIR_ROWS, PEER_TB), lambda i, h: (0, 0))],
        out_specs=[pl.BlockSpec((1, TOPK_HEADS * PEER_TOPK, PEER_TB), lambda i, h: (i, h, 0))] * 2,
        out_shape=[out, jax.ShapeDtypeStruct(out.shape, F32)],
        compiler_params=_params(("parallel", "arbitrary")),
        name="peer_topk",
    )(q, subkeys, pair_ranks())


def _gelu_tanh(x):
    return 0.5 * x * (1.0 + jnp.tanh(math.sqrt(2.0 / math.pi) * (x + 0.044715 * x * x * x)))


GATHER_DEPTH = 8
TOKENS_PER_STEP = 2
GATHER_TB = 2 * PEER_TB
LANES = 128


def _peer_gather_kernel(idx_hbm, gate_ref, h_ref, g2_ref, uv_hbm, o_ref, idx_ref, *scratch):
    bufs, sems = scratch[:GATHER_DEPTH], scratch[GATHER_DEPTH]
    blk = pl.program_id(0)
    idx_copy = pltpu.make_async_copy(idx_hbm.at[blk], idx_ref, sems.at[GATHER_DEPTH])
    idx_copy.start()
    idx_copy.wait()

    def start_rows(t, slot):
        base = t * PEER_PICKS
        for k in range(PEER_PICKS):
            e = idx_ref[base + k]
            pltpu.make_async_copy(uv_hbm.at[e], bufs[slot].at[pl.ds(k, 1)], sems.at[slot]).start(priority=k % 2)

    def wait_rows(slot):
        pltpu.make_async_copy(uv_hbm.at[pl.ds(0, PEER_PICKS), 0], bufs[slot], sems.at[slot]).wait()

    lane = lax.broadcasted_iota(I32, (PEER_PICKS, PEER_TB), 1)
    half = D_MODEL // 2

    def unpack(words):
        lo = lax.bitcast_convert_type(words << 16, F32)
        hi = lax.bitcast_convert_type(words & jnp.uint32(0xFFFF0000), F32)
        return lo, hi

    def compute(t, slot):
        buf = bufs[slot]
        hrow = h_ref[pl.ds(t, 1), :]
        xn = _rms(hrow, g2_ref[...])
        acc = jnp.zeros((PEER_PICKS, LANES), F32)
        for c in range(half // LANES):
            u_lo, u_hi = unpack(buf[:, c * LANES:(c + 1) * LANES])
            acc = acc + (u_lo * xn[:, c * LANES:(c + 1) * LANES]
                         + u_hi * xn[:, half + c * LANES:half + (c + 1) * LANES])
        act = _gelu_tanh(jnp.sum(acc, axis=1, keepdims=True))
        gate = jnp.sum(jnp.where(lane == (t & (PEER_TB - 1)), gate_ref[t >> PEER_TB.bit_length() - 1], 0.0),
                       axis=1, keepdims=True)
        w = jnp.broadcast_to(gate * act, (PEER_PICKS, LANES))
        mix_lo, mix_hi = [], []
        for c in range(half // LANES):
            v_lo, v_hi = unpack(buf[:, half + c * LANES:half + (c + 1) * LANES])
            mix_lo.append(jnp.sum(w * v_lo, axis=0, keepdims=True))
            mix_hi.append(jnp.sum(w * v_hi, axis=0, keepdims=True))
        o_ref[pl.ds(t, 1), :] = hrow + jnp.concatenate(mix_lo + mix_hi, axis=1)

    ahead = GATHER_DEPTH - TOKENS_PER_STEP

    def group(j, last):
        for b in range(0, GATHER_DEPTH, TOKENS_PER_STEP):
            t = j * GATHER_DEPTH + b
            for i in range(TOKENS_PER_STEP):
                wait_rows(b + i)
            for i in range(TOKENS_PER_STEP):
                if not last or b + i + ahead < GATHER_DEPTH:
                    start_rows(t + i + ahead, (b + i + ahead) % GATHER_DEPTH)
            for i in range(TOKENS_PER_STEP):
                compute(t + i, b + i)

    for b in range(ahead):
        start_rows(b, b)
    groups = GATHER_TB // GATHER_DEPTH

    def body(j, carry):
        group(j, False)
        return carry

    lax.fori_loop(0, groups - 1, body, 0)
    group(groups - 1, True)


def _bf16_bits_rounded(x):
    bits = lax.bitcast_convert_type(x, jnp.uint32)
    return bits + jnp.uint32(0x7FFF) + ((bits >> 16) & jnp.uint32(1))


def _pack_kernel(u_ref, v_ref, o_ref):
    half = u_ref.shape[2] // 2
    for src, base in ((u_ref, 0), (v_ref, half)):
        r = _bf16_bits_rounded(src[0])
        o_ref[:, 0, base:base + half] = (r[:, :half] >> 16) | (r[:, half:] & jnp.uint32(0xFFFF0000))


def pack_expert_tables(u, v, layer, *, tm=256):
    _, e, d = u.shape
    return pl.pallas_call(
        _pack_kernel,
        grid=(e // tm,),
        in_specs=[pl.BlockSpec((1, tm, d), lambda i: (layer, i, 0))] * 2,
        out_specs=pl.BlockSpec((tm, 1, d), lambda i: (i, 0, 0)),
        out_shape=jax.ShapeDtypeStruct((e, 1, d), jnp.uint32),
        compiler_params=_params(("parallel",)),
        name="pack_expert_tables",
    )(u, v)


def peer_gather(idx, gate, h, g2, uv):
    n, d = h.shape
    nblk = n // GATHER_TB
    return pl.pallas_call(
        _peer_gather_kernel,
        grid=(nblk,),
        in_specs=[pl.BlockSpec(memory_space=pl.ANY),
                  pl.BlockSpec((GATHER_TB // PEER_TB, PEER_PICKS, PEER_TB), lambda i: (i, 0, 0)),
                  pl.BlockSpec((GATHER_TB, d), lambda i: (i, 0)),
                  pl.BlockSpec((1, d), lambda i: (0, 0)),
                  pl.BlockSpec(memory_space=pl.ANY)],
        out_specs=pl.BlockSpec((GATHER_TB, d), lambda i: (i, 0)),
        out_shape=jax.ShapeDtypeStruct((n, d), F32),
        scratch_shapes=([pltpu.SMEM((GATHER_TB * PEER_PICKS,), I32)]
                        + [pltpu.VMEM((PEER_PICKS, d), jnp.uint32)] * GATHER_DEPTH
                        + [pltpu.SemaphoreType.DMA((GATHER_DEPTH + 1,))]),
        compiler_params=_params(("arbitrary",)),
        name="peer_gather",
    )(idx, gate, h, g2.reshape(1, d), uv)


def _rel_bucket(rel):
    nb = NUM_BUCKETS // 2
    max_exact = nb // 2
    n = jnp.abs(rel)
    large = max_exact + (jnp.log(jnp.maximum(n, 1).astype(F32) / max_exact)
                         / math.log(MAX_DISTANCE / max_exact) * (nb - max_exact)).astype(I32)
    large = jnp.minimum(large, nb - 1)
    return jnp.where(rel > 0, nb, 0) + jnp.where(n < max_exact, n, large)


def _bias(rel, table):
    nq, nk = rel.shape
    return jnp.transpose(table[_rel_bucket(rel)].astype(F32), (2, 0, 1)).reshape(N_KV_HEADS, GROUP * nq, nk)


def kernel(x_prompt, x_sample, cache_k, cache_v, state_conv, rel_bias, norm1_g, w_in, attn_sink, conv_w, conv_b,
           conv_ln_g, conv_ln_b, out_norm_attn_g, out_norm_conv_g, w_out, norm2_g, peer_wq, peer_subkeys, peer_u,
           peer_v, final_norm_g):
    batch, seq, d = x_prompt.shape
    dec_batch, dec_seq, _ = x_sample.shape
    depth = w_in.shape[0]
    n_p, n_s = batch * seq, dec_batch * dec_seq
    n = n_p + n_s
    h = jnp.concatenate([x_prompt.reshape(n_p, d), x_sample.reshape(n_s, d)], axis=0)

    qi = jnp.arange(CHUNK, dtype=I32)
    bias_p = _bias((jnp.arange(BAND, dtype=I32) - WINDOW)[None, :] - qi[:, None], rel_bias)
    ti = jnp.arange(dec_seq, dtype=I32)
    kpos_s = jnp.concatenate([jnp.arange(WINDOW, dtype=I32) - WINDOW, ti])
    bias_s = _bias(kpos_s[None, :] - ti[:, None], rel_bias)

    outs = {k: [] for k in ("kp", "vp", "cp", "ks", "vs", "cs")}
    for l in range(depth):
        wl = w_in[l]
        o1 = ATTN_WIDTH
        o2 = o1 + KV_WIDTH
        o3 = o2 + KV_WIDTH
        w_perm = jnp.concatenate([wl[:, :o1], wl[:, o3:], wl[:, o1:o3]], axis=1).astype(BF16)
        z = rms_matmul(h, norm1_g[l], w_perm, tn=IN_WIDTH // 4)

        att_p = attn_prompt(z, bias_p, attn_sink[l], batch, seq)
        att_s = attn_sample(z, cache_k[l].reshape(dec_batch, WINDOW, KV_WIDTH),
                            cache_v[l].reshape(dec_batch, WINDOW, KV_WIDTH), bias_s, attn_sink[l],
                            n_p, dec_batch, dec_seq)
        conv_p, tail_p = conv_prompt(z, conv_w[l], conv_b[l], conv_ln_g[l], conv_ln_b[l], batch, seq)
        state = jnp.pad(state_conv[l], ((0, 0), (CONV_HALO - CONV_STATE, 0), (0, 0)))
        conv_s, tail_s = conv_sample(z, state, conv_w[l], conv_b[l], conv_ln_g[l], conv_ln_b[l],
                                     n_p, dec_batch, dec_seq)
        h = merge(att_p, att_s, conv_p, conv_s, out_norm_attn_g[l], out_norm_conv_g[l], w_out[l].astype(BF16), h)

        q = rms_matmul(h, norm2_g[l], peer_wq[l].astype(BF16), tn=1024)
        idx_t, gate = peer_topk(q, peer_subkeys[l])
        idx = jnp.transpose(idx_t, (0, 2, 1)).reshape(n // GATHER_TB, GATHER_TB * PEER_PICKS)
        uv = pack_expert_tables(peer_u, peer_v, l)
        h = peer_gather(idx, gate, h, norm2_g[l], uv)

        kv_p = jnp.stack([lax.slice(z, ((b + 1) * seq - WINDOW, K_OFF), ((b + 1) * seq, IN_WIDTH))
                          for b in range(batch)])
        kv_s = lax.slice(z, (n_p, K_OFF), (n, IN_WIDTH)).reshape(dec_batch, dec_seq, 2 * KV_WIDTH)
        outs["kp"].append(kv_p[:, :, :KV_WIDTH].reshape(batch, WINDOW, N_KV_HEADS, HEAD_DIM))
        outs["vp"].append(kv_p[:, :, KV_WIDTH:].reshape(batch, WINDOW, N_KV_HEADS, HEAD_DIM))
        outs["cp"].append(tail_p[:, CONV_HALO - CONV_STATE:])
        outs["ks"].append(kv_s[:, :, :KV_WIDTH].reshape(dec_batch, dec_seq, N_KV_HEADS, HEAD_DIM))
        outs["vs"].append(kv_s[:, :, KV_WIDTH:].reshape(dec_batch, dec_seq, N_KV_HEADS, HEAD_DIM))
        outs["cs"].append(tail_s[:, CONV_HALO - CONV_STATE:])

    y_p, y_s = final_norm(h, final_norm_g, n_p)
    y_prompt = y_p.reshape(batch, seq, d)
    y_sample = y_s.reshape(dec_batch, dec_seq, d)
    return (y_prompt, y_sample, jnp.stack(outs["kp"]), jnp.stack(outs["vp"]), jnp.stack(outs["cp"]),
            jnp.stack(outs["ks"]), jnp.stack(outs["vs"]), jnp.stack(outs["cs"]))
```

```python
import functools
import math

import jax
import jax.numpy as jnp
from jax import lax
from jax.experimental import pallas as pl
from jax.experimental.pallas import tpu as pltpu

F32 = jnp.float32
BF16 = jnp.bfloat16
I32 = jnp.int32

D_MODEL = 2048
CHUNK = 64
ATTN_WIDTH = 1024
CONV_CH = 1024
HEAD_DIM = 64
N_HEADS = 16
N_KV_HEADS = 4
GROUP = N_HEADS // N_KV_HEADS
KV_WIDTH = N_KV_HEADS * HEAD_DIM
WINDOW = 128
BAND = WINDOW + CHUNK
CONV_WIDTH = 31
CONV_STATE = CONV_WIDTH - 1
CONV_HALO = 32
NUM_BUCKETS = 32
MAX_DISTANCE = 128
PEER_HEADS = 8
PEER_NKEYS = 128
PEER_HALF = 128
PEER_TOPK = 16
PEER_PICKS = PEER_HEADS * PEER_TOPK
IN_WIDTH = ATTN_WIDTH + 2 * KV_WIDTH + 2 * CONV_CH
EPS = 1e-6
ATTN_SCALE = 1.0 / math.sqrt(HEAD_DIM)
NEG_INF = -1e30

Q_OFF = 0
A_OFF = ATTN_WIDTH
G_OFF = A_OFF + CONV_CH
K_OFF = G_OFF + CONV_CH
V_OFF = K_OFF + KV_WIDTH

VMEM_LIMIT = 56 * 1024 * 1024


def _params(sem, vmem=VMEM_LIMIT):
    return pltpu.CompilerParams(dimension_semantics=sem, vmem_limit_bytes=vmem)


def _rms(x, g):
    return x * lax.rsqrt(jnp.mean(x * x, axis=-1, keepdims=True) + EPS) * g


def _rms_matmul_kernel(x_ref, g_ref, w_ref, o_ref, xn_ref):
    @pl.when(pl.program_id(1) == 0)
    def _():
        xn_ref[...] = _rms(x_ref[...], g_ref[...]).astype(BF16)

    o_ref[...] = jnp.dot(xn_ref[...], w_ref[...], preferred_element_type=F32)


def rms_matmul(x, g, w, *, tm=512, tn):
    n, d = x.shape
    m = w.shape[1]
    return pl.pallas_call(
        _rms_matmul_kernel,
        grid=(n // tm, m // tn),
        in_specs=[pl.BlockSpec((tm, d), lambda i, j: (i, 0)),
                  pl.BlockSpec((1, d), lambda i, j: (0, 0)),
                  pl.BlockSpec((d, tn), lambda i, j: (0, j))],
        out_specs=pl.BlockSpec((tm, tn), lambda i, j: (i, j)),
        out_shape=jax.ShapeDtypeStruct((n, m), F32),
        scratch_shapes=[pltpu.VMEM((tm, d), BF16)],
        compiler_params=_params(("parallel", "arbitrary")),
        name="rms_matmul",
    )(x, g.reshape(1, d), w)


def _attn_heads(q, kk, vv, bias_ref, sink_ref, mask):
    tq = q.shape[0]
    row = lax.broadcasted_iota(I32, (GROUP * tq, 1), 0)
    scores, sinks = [], []
    for g in range(N_KV_HEADS):
        qg = jnp.concatenate([q[:, (g * GROUP + j) * HEAD_DIM:(g * GROUP + j + 1) * HEAD_DIM]
                              for j in range(GROUP)], axis=0)
        kh = kk[:, g * HEAD_DIM:(g + 1) * HEAD_DIM]
        s = lax.dot_general(qg, kh, (((1,), (1,)), ((), ())), preferred_element_type=F32)
        s = s * ATTN_SCALE + bias_ref[g]
        if mask is not None:
            s = jnp.where(mask, s, NEG_INF)
        scores.append(s)
        sk = jnp.full((GROUP * tq, 1), sink_ref[g * GROUP + GROUP - 1], F32)
        for j in range(GROUP - 2, -1, -1):
            sk = jnp.where(row < (j + 1) * tq, sink_ref[g * GROUP + j], sk)
        sinks.append(sk)
    probs = []
    for s, sk in zip(scores, sinks):
        m = jnp.maximum(jnp.max(s, axis=-1, keepdims=True), sk)
        p = jnp.exp(s - m)
        probs.append(p * (1.0 / (jnp.sum(p, axis=-1, keepdims=True) + jnp.exp(sk - m))))
    outs = []
    for g, p in enumerate(probs):
        o = jnp.dot(p, vv[:, g * HEAD_DIM:(g + 1) * HEAD_DIM], preferred_element_type=F32)
        outs += [o[j * tq:(j + 1) * tq, :] for j in range(GROUP)]
    return jnp.concatenate(outs, axis=1)


def _attn_prompt_kernel(q_ref, k0_ref, k1_ref, k2_ref, v0_ref, v1_ref, v2_ref, bias_ref, sink_ref, o_ref):
    c = pl.program_id(1)
    kk = jnp.concatenate([k0_ref[...], k1_ref[...], k2_ref[...]], axis=0)
    vv = jnp.concatenate([v0_ref[...], v1_ref[...], v2_ref[...]], axis=0)
    col = lax.broadcasted_iota(I32, (1, BAND), 1)
    mask = col + c * CHUNK - WINDOW >= 0
    o_ref[...] = _attn_heads(q_ref[...], kk, vv, bias_ref, sink_ref, mask)


def attn_prompt(z, bias, sink, batch, seq):
    n_chunks = seq // CHUNK
    kb, vb = K_OFF // KV_WIDTH, V_OFF // KV_WIDTH

    def band_spec(back, colblock):
        return pl.BlockSpec((CHUNK, KV_WIDTH),
                            lambda b, c: (b * n_chunks + jnp.maximum(c - back, 0), colblock))

    return pl.pallas_call(
        _attn_prompt_kernel,
        grid=(batch, n_chunks),
        in_specs=[pl.BlockSpec((CHUNK, ATTN_WIDTH), lambda b, c: (b * n_chunks + c, 0)),
                  band_spec(2, kb), band_spec(1, kb), band_spec(0, kb),
                  band_spec(2, vb), band_spec(1, vb), band_spec(0, vb),
                  pl.BlockSpec((N_KV_HEADS, GROUP * CHUNK, BAND), lambda b, c: (0, 0, 0)),
                  pl.BlockSpec(memory_space=pltpu.SMEM)],
        out_specs=pl.BlockSpec((CHUNK, ATTN_WIDTH), lambda b, c: (b * n_chunks + c, 0)),
        out_shape=jax.ShapeDtypeStruct((batch * seq, ATTN_WIDTH), F32),
        compiler_params=_params(("parallel", "arbitrary")),
        name="attn_prompt",
    )(z, z, z, z, z, z, z, bias, sink)


def _attn_sample_kernel(q_ref, kn_ref, vn_ref, kc_ref, vc_ref, bias_ref, sink_ref, o_ref):
    kk = jnp.concatenate([kc_ref[0], kn_ref[...]], axis=0)
    vv = jnp.concatenate([vc_ref[0], vn_ref[...]], axis=0)
    o_ref[...] = _attn_heads(q_ref[...], kk, vv, bias_ref, sink_ref, None)


def attn_sample(z, cache_k, cache_v, bias, sink, row0, dec_batch, dec_seq):
    r0 = row0 // dec_seq
    kb, vb = K_OFF // KV_WIDTH, V_OFF // KV_WIDTH
    return pl.pallas_call(
        _attn_sample_kernel,
        grid=(dec_batch,),
        in_specs=[pl.BlockSpec((dec_seq, ATTN_WIDTH), lambda b: (r0 + b, 0)),
                  pl.BlockSpec((dec_seq, KV_WIDTH), lambda b: (r0 + b, kb)),
                  pl.BlockSpec((dec_seq, KV_WIDTH), lambda b: (r0 + b, vb)),
                  pl.BlockSpec((1, WINDOW, KV_WIDTH), lambda b: (b, 0, 0)),
                  pl.BlockSpec((1, WINDOW, KV_WIDTH), lambda b: (b, 0, 0)),
                  pl.BlockSpec((N_KV_HEADS, GROUP * dec_seq, WINDOW + dec_seq), lambda b: (0, 0, 0)),
                  pl.BlockSpec(memory_space=pltpu.SMEM)],
        out_specs=pl.BlockSpec((dec_seq, ATTN_WIDTH), lambda b: (b, 0)),
        out_shape=jax.ShapeDtypeStruct((dec_batch * dec_seq, ATTN_WIDTH), F32),
        compiler_params=_params(("parallel",)),
        name="attn_sample",
    )(z, z, z, cache_k, cache_v, bias, sink)


CONV_ROWS = 32
SUBLANES = 8
CONV_SHIFT_PAD = CONV_HALO - SUBLANES


def _conv_kernel(pa_ref, pg_ref, a_ref, g_ref, w_ref, cb_ref, lg_ref, lb_ref, o_ref, tail_ref, ext_ref, sh_ref,
                 *, rows, prev_is_glu):
    if prev_is_glu:
        prev = pa_ref[0]
    else:
        prev = pa_ref[...] * jax.nn.sigmoid(pg_ref[...])
        prev = jnp.where(pl.program_id(1) == 0, 0.0, prev)
    ext_ref[0:CONV_HALO, :] = prev
    ext_ref[CONV_HALO:CONV_HALO + rows, :] = a_ref[...] * jax.nn.sigmoid(g_ref[...])
    span = rows + CONV_SHIFT_PAD
    for s in range(1, SUBLANES):
        sh_ref[s - 1] = ext_ref[s:s + span, :]
    step = min(CONV_ROWS, rows)
    for r0 in range(0, rows, step):
        acc = jnp.zeros((step, CONV_CH), F32)
        for j in range(CONV_WIDTH):
            off = r0 + j + CONV_HALO - CONV_STATE
            shift, base = off % SUBLANES, off - off % SUBLANES
            window = ext_ref[base:base + step, :] if shift == 0 else sh_ref[shift - 1, base:base + step, :]
            acc = acc + w_ref[j:j + 1, :] * window
        y = acc + cb_ref[...]
        mu = jnp.mean(y, axis=-1, keepdims=True)
        yc = y - mu
        yn = yc * lax.rsqrt(jnp.mean(yc * yc, axis=-1, keepdims=True) + EPS) * lg_ref[...] + lb_ref[...]
        o_ref[r0:r0 + step, :] = yn * jax.nn.sigmoid(yn)
    tail_ref[0] = ext_ref[rows:rows + CONV_HALO, :]


def _conv_call(kernel, grid, in_specs, out_specs, out_shape, rows, sem, name, args):
    return pl.pallas_call(
        kernel, grid=grid, in_specs=in_specs, out_specs=out_specs, out_shape=out_shape,
        scratch_shapes=[pltpu.VMEM((CONV_HALO + rows, CONV_CH), F32),
                        pltpu.VMEM((SUBLANES - 1, rows + CONV_SHIFT_PAD, CONV_CH), F32)],
        compiler_params=_params(sem), name=name,
    )(*args)


def _conv_param_specs(nd):
    zero = (lambda b, t: (0, 0)) if nd == 2 else (lambda b: (0, 0))
    return [pl.BlockSpec((CONV_WIDTH, CONV_CH), zero)] + [pl.BlockSpec((1, CONV_CH), zero)] * 3


def conv_prompt(z, w, cb, lg, lb, batch, seq, *, rows=256):
    tiles = seq // rows
    halo_per_tile = rows // CONV_HALO
    ab, gb = A_OFF // CONV_CH, G_OFF // CONV_CH

    def prev_spec(colblock):
        return pl.BlockSpec(
            (CONV_HALO, CONV_CH),
            lambda b, t: (jnp.maximum((b * tiles + t) * halo_per_tile - 1, 0), colblock))

    def cur_spec(colblock):
        return pl.BlockSpec((rows, CONV_CH), lambda b, t: (b * tiles + t, colblock))

    return _conv_call(
        functools.partial(_conv_kernel, rows=rows, prev_is_glu=False),
        (batch, tiles),
        [prev_spec(ab), prev_spec(gb), cur_spec(ab), cur_spec(gb)] + _conv_param_specs(2),
        [pl.BlockSpec((rows, CONV_CH), lambda b, t: (b * tiles + t, 0)),
         pl.BlockSpec((1, CONV_HALO, CONV_CH), lambda b, t: (b, 0, 0))],
        [jax.ShapeDtypeStruct((batch * seq, CONV_CH), F32),
         jax.ShapeDtypeStruct((batch, CONV_HALO, CONV_CH), F32)],
        rows, ("parallel", "arbitrary"), "conv_prompt",
        (z, z, z, z, w, cb.reshape(1, -1), lg.reshape(1, -1), lb.reshape(1, -1)))


def conv_sample(z, state, w, cb, lg, lb, row0, dec_batch, dec_seq):
    r0 = row0 // dec_seq
    ab, gb = A_OFF // CONV_CH, G_OFF // CONV_CH
    st_spec = pl.BlockSpec((1, CONV_HALO, CONV_CH), lambda b: (b, 0, 0))
    return _conv_call(
        functools.partial(_conv_kernel, rows=dec_seq, prev_is_glu=True),
        (dec_batch,),
        [st_spec, st_spec,
         pl.BlockSpec((dec_seq, CONV_CH), lambda b: (r0 + b, ab)),
         pl.BlockSpec((dec_seq, CONV_CH), lambda b: (r0 + b, gb))] + _conv_param_specs(1),
        [pl.BlockSpec((dec_seq, CONV_CH), lambda b: (b, 0)),
         pl.BlockSpec((1, CONV_HALO, CONV_CH), lambda b: (b, 0, 0))],
        [jax.ShapeDtypeStruct((dec_batch * dec_seq, CONV_CH), F32),
         jax.ShapeDtypeStruct((dec_batch, CONV_HALO, CONV_CH), F32)],
        dec_seq, ("parallel",), "conv_sample",
        (state, state, z, z, w, cb.reshape(1, -1), lg.reshape(1, -1), lb.reshape(1, -1)))


def _merge_kernel(ap_ref, as_ref, cp_ref, cs_ref, ga_ref, gc_ref, wa_ref, wc_ref, res_ref, o_ref, an_ref, cn_ref,
                  *, prompt_tiles):
    first = pl.program_id(1) == 0
    is_prompt = pl.program_id(0) < prompt_tiles

    def normalise(att_ref, conv_ref):
        an_ref[...] = _rms(att_ref[...], ga_ref[...]).astype(BF16)
        cn_ref[...] = _rms(conv_ref[...], gc_ref[...]).astype(BF16)

    pl.when(first & is_prompt)(lambda: normalise(ap_ref, cp_ref))
    pl.when(first & jnp.logical_not(is_prompt))(lambda: normalise(as_ref, cs_ref))
    o_ref[...] = (res_ref[...] + jnp.dot(an_ref[...], wa_ref[...], preferred_element_type=F32)
                  + jnp.dot(cn_ref[...], wc_ref[...], preferred_element_type=F32))


def merge(att_p, att_s, conv_p, conv_s, ga, gc, w_out, res, *, tm=512, tn=1024):
    n = res.shape[0]
    d = w_out.shape[1]
    prompt_tiles = att_p.shape[0] // tm

    def rows_p(i, j):
        return (jnp.minimum(i, prompt_tiles - 1), 0)

    def rows_s(i, j):
        return (jnp.maximum(i - prompt_tiles, 0), 0)

    return pl.pallas_call(
        functools.partial(_merge_kernel, prompt_tiles=prompt_tiles),
        grid=(n // tm, d // tn),
        in_specs=[pl.BlockSpec((tm, ATTN_WIDTH), rows_p),
                  pl.BlockSpec((tm, ATTN_WIDTH), rows_s),
                  pl.BlockSpec((tm, CONV_CH), rows_p),
                  pl.BlockSpec((tm, CONV_CH), rows_s),
                  pl.BlockSpec((1, ATTN_WIDTH), lambda i, j: (0, 0)),
                  pl.BlockSpec((1, CONV_CH), lambda i, j: (0, 0)),
                  pl.BlockSpec((ATTN_WIDTH, tn), lambda i, j: (0, j)),
                  pl.BlockSpec((CONV_CH, tn), lambda i, j: (1, j)),
                  pl.BlockSpec((tm, tn), lambda i, j: (i, j))],
        out_specs=pl.BlockSpec((tm, tn), lambda i, j: (i, j)),
        out_shape=jax.ShapeDtypeStruct((n, d), F32),
        scratch_shapes=[pltpu.VMEM((tm, ATTN_WIDTH), BF16), pltpu.VMEM((tm, CONV_CH), BF16)],
        compiler_params=_params(("parallel", "arbitrary")),
        name="merge",
    )(att_p, att_s, conv_p, conv_s, ga.reshape(1, -1), gc.reshape(1, -1), w_out, w_out, res)


def _final_norm_kernel(x_ref, g_ref, op_ref, os_ref, *, prompt_tiles):
    y = _rms(x_ref[...], g_ref[...])
    is_prompt = pl.program_id(0) < prompt_tiles

    @pl.when(is_prompt)
    def _():
        op_ref[...] = y

    @pl.when(jnp.logical_not(is_prompt))
    def _():
        os_ref[...] = y


def final_norm(x, g, n_p, *, tm=512):
    n, d = x.shape
    prompt_tiles = n_p // tm
    return pl.pallas_call(
        functools.partial(_final_norm_kernel, prompt_tiles=prompt_tiles),
        grid=(n // tm,),
        in_specs=[pl.BlockSpec((tm, d), lambda i: (i, 0)), pl.BlockSpec((1, d), lambda i: (0, 0))],
        out_specs=[pl.BlockSpec((tm, d), lambda i: (jnp.minimum(i, prompt_tiles - 1), 0)),
                   pl.BlockSpec((tm, d), lambda i: (jnp.maximum(i - prompt_tiles, 0), 0))],
        out_shape=[jax.ShapeDtypeStruct((n_p, d), F32), jax.ShapeDtypeStruct((n - n_p, d), F32)],
        compiler_params=_params(("arbitrary",)),
        name="final_norm",
    )(x, g.reshape(1, d))


PEER_TB = 128


def _topk_rows(chains, k, rank, payloads=None):
    chains = list(chains)
    vals = [[] for _ in chains]
    picks = [[] for _ in chains]
    for _ in range(k):
        for c, s in enumerate(chains):
            m = jnp.max(s, axis=0, keepdims=True)
            first = jnp.min(jnp.where(s == m, rank, RANK_NONE), axis=0, keepdims=True)
            hit = rank == first
            vals[c].append(m)
            picks[c].append(first if payloads is None
                            else jnp.max(jnp.where(hit, payloads[c], -1.0), axis=0, keepdims=True))
            chains[c] = jnp.where(hit, -jnp.inf, s)
    return ([jnp.concatenate(v, axis=0) for v in vals], [jnp.concatenate(p, axis=0) for p in picks])


RANK_NONE = 1e9

PAIR_SEGMENTS = ((0, 1, 0, 16), (1, 1, 0, 8), (2, 1, 0, 8), (3, 1, 0, 8),
                 (8, 8, 0, 1), (4, 4, 0, 1), (4, 4, 1, 1), (4, 1, 2, 1))
PAIR_ROWS = 64


def _pair_rows(a, b, combine):
    return [combine(a[i0:i0 + ni, :], b[j0:j0 + nj, :]) for i0, ni, j0, nj in PAIR_SEGMENTS]


def pair_ranks():
    rows = []
    for i0, ni, j0, nj in PAIR_SEGMENTS:
        rows += [float(i * PEER_TOPK + j) for i in range(i0, i0 + ni) for j in range(j0, j0 + nj)]
    rows += [RANK_NONE] * (PAIR_ROWS - len(rows))
    return jnp.broadcast_to(jnp.asarray(rows, F32)[:, None], (PAIR_ROWS, PEER_TB))


TOPK_HEADS = 4


def _peer_topk_kernel(q_ref, keys_ref, prank_ref, idx_ref, gate_ref):
    q = q_ref[...]
    dn = (((1,), (1,)), ((), ()))
    scores = [lax.dot_general(keys_ref[c % 2], q[:, c * PEER_HALF:(c + 1) * PEER_HALF], dn,
                              preferred_element_type=F32) for c in range(2 * TOPK_HEADS)]
    key_rank = lax.broadcasted_iota(I32, scores[0].shape, 0).astype(F32)
    vals, keys = _topk_rows(scores, PEER_TOPK, key_rank)
    pad = PAIR_ROWS - sum(ni * nj for _, ni, _, nj in PAIR_SEGMENTS)
    fill = jnp.full((pad, PEER_TB), -jnp.inf, F32)
    cand = [jnp.concatenate(_pair_rows(vals[2 * h], vals[2 * h + 1], lambda x, y: x + y) + [fill], axis=0)
            for h in range(TOPK_HEADS)]
    expert = [jnp.concatenate(_pair_rows(keys[2 * h], keys[2 * h + 1], lambda x, y: x * PEER_NKEYS + y) + [fill],
                              axis=0) for h in range(TOPK_HEADS)]
    cvs, picks = _topk_rows(cand, PEER_TOPK, prank_ref[...], payloads=expert)
    for h in range(TOPK_HEADS):
        e = jnp.exp(cvs[h] - cvs[h][0:1, :])
        idx_ref[0, h * PEER_TOPK:(h + 1) * PEER_TOPK, :] = picks[h].astype(I32)
        gate_ref[0, h * PEER_TOPK:(h + 1) * PEER_TOPK, :] = e / jnp.sum(e, axis=0, keepdims=True)


def peer_topk(q, subkeys):
    n = q.shape[0]
    nblk = n // PEER_TB
    out = jax.ShapeDtypeStruct((nblk, PEER_PICKS, PEER_TB), I32)
    return pl.pallas_call(
        _peer_topk_kernel,
        grid=(nblk, PEER_HEADS // TOPK_HEADS),
        in_specs=[pl.BlockSpec((PEER_TB, TOPK_HEADS * 2 * PEER_HALF), lambda i, h: (i, h)),
                  pl.BlockSpec((2, PEER_NKEYS, PEER_HALF), lambda i, h: (0, 0, 0)),
                  pl.BlockSpec((PAIR_ROWS, PEER_TB), lambda i, h: (0, 0))],
        out_specs=[pl.BlockSpec((1, TOPK_HEADS * PEER_TOPK, PEER_TB), lambda i, h: (i, h, 0))] * 2,
        out_shape=[out, jax.ShapeDtypeStruct(out.shape, F32)],
        compiler_params=_params(("parallel", "arbitrary")),
        name="peer_topk",
    )(q, subkeys, pair_ranks())


def _gelu_tanh(x):
    return 0.5 * x * (1.0 + jnp.tanh(math.sqrt(2.0 / math.pi) * (x + 0.044715 * x * x * x)))


GATHER_DEPTH = 8
TOKENS_PER_STEP = 2
GATHER_TB = 2 * PEER_TB
LANES = 128


def _peer_gather_kernel(idx_hbm, gate_ref, h_ref, g2_ref, uv_hbm, o_ref, idx_ref, *scratch):
    bufs, sems = scratch[:GATHER_DEPTH], scratch[GATHER_DEPTH]
    blk = pl.program_id(0)
    idx_copy = pltpu.make_async_copy(idx_hbm.at[blk], idx_ref, sems.at[GATHER_DEPTH])
    idx_copy.start()
    idx_copy.wait()

    def start_rows(t, slot):
        base = t * PEER_PICKS
        for k in range(PEER_PICKS):
            e = idx_ref[base + k]
            pltpu.make_async_copy(uv_hbm.at[e], bufs[slot].at[pl.ds(k, 1)], sems.at[slot]).start(priority=k % 2)

    def wait_rows(slot):
        pltpu.make_async_copy(uv_hbm.at[pl.ds(0, PEER_PICKS), 0], bufs[slot], sems.at[slot]).wait()

    lane = lax.broadcasted_iota(I32, (PEER_PICKS, PEER_TB), 1)
    half = D_MODEL // 2

    def unpack(words):
        lo = lax.bitcast_convert_type(words << 16, F32)
        hi = lax.bitcast_convert_type(words & jnp.uint32(0xFFFF0000), F32)
        return lo, hi

    def compute(t, slot):
        buf = bufs[slot]
        hrow = h_ref[pl.ds(t, 1), :]
        xn = _rms(hrow, g2_ref[...])
        acc = jnp.zeros((PEER_PICKS, LANES), F32)
        for c in range(half // LANES):
            u_lo, u_hi = unpack(buf[:, c * LANES:(c + 1) * LANES])
            acc = acc + (u_lo * xn[:, c * LANES:(c + 1) * LANES]
                         + u_hi * xn[:, half + c * LANES:half + (c + 1) * LANES])
        act = _gelu_tanh(jnp.sum(acc, axis=1, keepdims=True))
        gate = jnp.sum(jnp.where(lane == (t & (PEER_TB - 1)), gate_ref[t >> PEER_TB.bit_length() - 1], 0.0),
                       axis=1, keepdims=True)
        w = jnp.broadcast_to(gate * act, (PEER_PICKS, LANES))
        mix_lo, mix_hi = [], []
        for c in range(half // LANES):
            v_lo, v_hi = unpack(buf[:, half + c * LANES:half + (c + 1) * LANES])
            mix_lo.append(jnp.sum(w * v_lo, axis=0, keepdims=True))
            mix_hi.append(jnp.sum(w * v_hi, axis=0, keepdims=True))
        o_ref[pl.ds(t, 1), :] = hrow + jnp.concatenate(mix_lo + mix_hi, axis=1)

    ahead = GATHER_DEPTH - TOKENS_PER_STEP

    def group(j, last):
        for b in range(0, GATHER_DEPTH, TOKENS_PER_STEP):
            t = j * GATHER_DEPTH + b
            for i in range(TOKENS_PER_STEP):
                wait_rows(b + i)
            for i in range(TOKENS_PER_STEP):
                if not last or b + i + ahead < GATHER_DEPTH:
                    start_rows(t + i + ahead, (b + i + ahead) % GATHER_DEPTH)
            for i in range(TOKENS_PER_STEP):
                compute(t + i, b + i)

    for b in range(ahead):
        start_rows(b, b)
    groups = GATHER_TB // GATHER_DEPTH

    def body(j, carry):
        group(j, False)
        return carry

    lax.fori_loop(0, groups - 1, body, 0)
    group(groups - 1, True)


def _bf16_bits_rounded(x):
    bits = lax.bitcast_convert_type(x, jnp.uint32)
    return bits + jnp.uint32(0x7FFF) + ((bits >> 16) & jnp.uint32(1))


def _pack_kernel(u_ref, v_ref, o_ref):
    half = u_ref.shape[2] // 2
    for src, base in ((u_ref, 0), (v_ref, half)):
        r = _bf16_bits_rounded(src[0])
        o_ref[:, 0, base:base + half] = (r[:, :half] >> 16) | (r[:, half:] & jnp.uint32(0xFFFF0000))


def pack_expert_tables(u, v, layer, *, tm=256):
    _, e, d = u.shape
    return pl.pallas_call(
        _pack_kernel,
        grid=(e // tm,),
        in_specs=[pl.BlockSpec((1, tm, d), lambda i: (layer, i, 0))] * 2,
        out_specs=pl.BlockSpec((tm, 1, d), lambda i: (i, 0, 0)),
        out_shape=jax.ShapeDtypeStruct((e, 1, d), jnp.uint32),
        compiler_params=_params(("parallel",)),
        name="pack_expert_tables",
    )(u, v)


def peer_gather(idx, gate, h, g2, uv):
    n, d = h.shape
    nblk = n // GATHER_TB
    return pl.pallas_call(
        _peer_gather_kernel,
        grid=(nblk,),
        in_specs=[pl.BlockSpec(memory_space=pl.ANY),
                  pl.BlockSpec((GATHER_TB // PEER_TB, PEER_PICKS, PEER_TB), lambda i: (i, 0, 0)),
                  pl.BlockSpec((GATHER_TB, d), lambda i: (i, 0)),
                  pl.BlockSpec((1, d), lambda i: (0, 0)),
                  pl.BlockSpec(memory_space=pl.ANY)],
        out_specs=pl.BlockSpec((GATHER_TB, d), lambda i: (i, 0)),
        out_shape=jax.ShapeDtypeStruct((n, d), F32),
        scratch_shapes=([pltpu.SMEM((GATHER_TB * PEER_PICKS,), I32)]
                        + [pltpu.VMEM((PEER_PICKS, d), jnp.uint32)] * GATHER_DEPTH
                        + [pltpu.SemaphoreType.DMA((GATHER_DEPTH + 1,))]),
        compiler_params=_params(("arbitrary",)),
        name="peer_gather",
    )(idx, gate, h, g2.reshape(1, d), uv)


def _rel_bucket(rel):
    nb = NUM_BUCKETS // 2
    max_exact = nb // 2
    n = jnp.abs(rel)
    large = max_exact + (jnp.log(jnp.maximum(n, 1).astype(F32) / max_exact)
                         / math.log(MAX_DISTANCE / max_exact) * (nb - max_exact)).astype(I32)
    large = jnp.minimum(large, nb - 1)
    return jnp.where(rel > 0, nb, 0) + jnp.where(n < max_exact, n, large)


def _bias(rel, table):
    nq, nk = rel.shape
    return jnp.transpose(table[_rel_bucket(rel)].astype(F32), (2, 0, 1)).reshape(N_KV_HEADS, GROUP * nq, nk)


def kernel(x_prompt, x_sample, cache_k, cache_v, state_conv, rel_bias, norm1_g, w_in, attn_sink, conv_w, conv_b,
           conv_ln_g, conv_ln_b, out_norm_attn_g, out_norm_conv_g, w_out, norm2_g, peer_wq, peer_subkeys, peer_u,
           peer_v, final_norm_g):
    batch, seq, d = x_prompt.shape
    dec_batch, dec_seq, _ = x_sample.shape
    depth = w_in.shape[0]
    n_p, n_s = batch * seq, dec_batch * dec_seq
    n = n_p + n_s
    h = jnp.concatenate([x_prompt.reshape(n_p, d), x_sample.reshape(n_s, d)], axis=0)

    qi = jnp.arange(CHUNK, dtype=I32)
    bias_p = _bias((jnp.arange(BAND, dtype=I32) - WINDOW)[None, :] - qi[:, None], rel_bias)
    ti = jnp.arange(dec_seq, dtype=I32)
    kpos_s = jnp.concatenate([jnp.arange(WINDOW, dtype=I32) - WINDOW, ti])
    bias_s = _bias(kpos_s[None, :] - ti[:, None], rel_bias)

    outs = {k: [] for k in ("kp", "vp", "cp", "ks", "vs", "cs")}
    for l in range(depth):
        wl = w_in[l]
        o1 = ATTN_WIDTH
        o2 = o1 + KV_WIDTH
        o3 = o2 + KV_WIDTH
        w_perm = jnp.concatenate([wl[:, :o1], wl[:, o3:], wl[:, o1:o3]], axis=1).astype(BF16)
        z = rms_matmul(h, norm1_g[l], w_perm, tn=IN_WIDTH // 4)

        att_p = attn_prompt(z, bias_p, attn_sink[l], batch, seq)
        att_s = attn_sample(z, cache_k[l].reshape(dec_batch, WINDOW, KV_WIDTH),
                            cache_v[l].reshape(dec_batch, WINDOW, KV_WIDTH), bias_s, attn_sink[l],
                            n_p, dec_batch, dec_seq)
        conv_p, tail_p = conv_prompt(z, conv_w[l], conv_b[l], conv_ln_g[l], conv_ln_b[l], batch, seq)
        state = jnp.pad(state_conv[l], ((0, 0), (CONV_HALO - CONV_STATE, 0), (0, 0)))
        conv_s, tail_s = conv_sample(z, state, conv_w[l], conv_b[l], conv_ln_g[l], conv_ln_b[l],
                                     n_p, dec_batch, dec_seq)
        h = merge(att_p, att_s, conv_p, conv_s, out_norm_attn_g[l], out_norm_conv_g[l], w_out[l].astype(BF16), h)

        q = rms_matmul(h, norm2_g[l], peer_wq[l].astype(BF16), tn=1024)
        idx_t, gate = peer_topk(q, peer_subkeys[l])
        idx = jnp.transpose(idx_t, (0, 2, 1)).reshape(n // GATHER_TB, GATHER_TB * PEER_PICKS)
        uv = pack_expert_tables(peer_u, peer_v, l)
        h = peer_gather(idx, gate, h, norm2_g[l], uv)

        kv_p = jnp.stack([lax.slice(z, ((b + 1) * seq - WINDOW, K_OFF), ((b + 1) * seq, IN_WIDTH))
                          for b in range(batch)])
        kv_s = lax.slice(z, (n_p, K_OFF), (n, IN_WIDTH)).reshape(dec_batch, dec_seq, 2 * KV_WIDTH)
        outs["kp"].append(kv_p[:, :, :KV_WIDTH].reshape(batch, WINDOW, N_KV_HEADS, HEAD_DIM))
        outs["vp"].append(kv_p[:, :, KV_WIDTH:].reshape(batch, WINDOW, N_KV_HEADS, HEAD_DIM))
        outs["cp"].append(tail_p[:, CONV_HALO - CONV_STATE:])
        outs["ks"].append(kv_s[:, :, :KV_WIDTH].reshape(dec_batch, dec_seq, N_KV_HEADS, HEAD_DIM))
        outs["vs"].append(kv_s[:, :, KV_WIDTH:].reshape(dec_batch, dec_seq, N_KV_HEADS, HEAD_DIM))
        outs["cs"].append(tail_s[:, CONV_HALO - CONV_STATE:])

    y_p, y_s = final_norm(h, final_norm_g, n_p)
    y_prompt = y_p.reshape(batch, seq, d)
    y_sample = y_s.reshape(dec_batch, dec_seq, d)
    return (y_prompt, y_sample, jnp.stack(outs["kp"]), jnp.stack(outs["vp"]), jnp.stack(outs["cp"]),
            jnp.stack(outs["ks"]), jnp.stack(outs["vs"]), jnp.stack(outs["cs"]))
```
